```python
import math
import jax, jax.numpy as jnp
from jax import lax
import numpy as np

D_MODEL = 1024
BATCH = 2
SEQ = 16384
DEPTH = 1

GRID_W = 64
HEAD_DIM = 64
N_Q_HEADS = 8
N_KV_HEADS = 2
Q_PER_KV = N_Q_HEADS // N_KV_HEADS
ATTN_WIDTH = N_Q_HEADS * HEAD_DIM
KV_WIDTH = N_KV_HEADS * HEAD_DIM
Q_BLOCK = 128
ROPE_THETA = 10000.0
AXIAL_DIM = HEAD_DIM // 2
SSD_D_INNER = D_MODEL
SSD_HEADDIM = 64
SSD_HEADS = SSD_D_INNER // SSD_HEADDIM
SSD_GROUPS = 2
SSD_HEADS_PER_GROUP = SSD_HEADS // SSD_GROUPS
SSD_STATE = 128
CONV_K = 5
CHUNK = 128
CONV_CH = SSD_D_INNER + 2 * SSD_GROUPS * SSD_STATE
N_BRANCH = 2
D_FF = ((8 * D_MODEL // 3 + 255) // 256) * 256
EPS = 1e-6
IN_SPLITS = (ATTN_WIDTH, KV_WIDTH, KV_WIDTH, SSD_D_INNER, CONV_CH, 2 * SSD_HEADS, N_BRANCH * D_MODEL)
D_IN_PROJ = sum(IN_SPLITS)

kernel_name = "hybrid_gqa_ssd_gated_sandwich_block"


def rms_norm(x, g):
    xf = x.astype(jnp.float32)
    xf = xf * lax.rsqrt(jnp.mean(xf * xf, axis=-1, keepdims=True) + EPS)
    return (xf * g.astype(jnp.float32)).astype(x.dtype)


def split_cols(h, sizes):
    idx = np.cumsum(np.array(sizes[:-1]))
    return jnp.split(h, [int(i) for i in idx], axis=-1)


def axial_angles(seq_len):
    rows = seq_len // GRID_W
    row_idx = jnp.repeat(jnp.arange(rows, dtype=jnp.float32), GRID_W)
    col_idx = jnp.tile(jnp.arange(GRID_W, dtype=jnp.float32), rows)
    inv_freq = ROPE_THETA ** (-jnp.arange(0, AXIAL_DIM, 2, dtype=jnp.float32) / AXIAL_DIM)
    ang_row = row_idx[:, None] * inv_freq[None, :]
    ang_col = col_idx[:, None] * inv_freq[None, :]
    return ang_row, ang_col


def rotate(xh, ang):
    half = xh.shape[-1] // 2
    x1, x2 = xh[..., :half], xh[..., half:]
    cos, sin = jnp.cos(ang).astype(xh.dtype), jnp.sin(ang).astype(xh.dtype)
    return jnp.concatenate([x1 * cos - x2 * sin, x2 * cos + x1 * sin], axis=-1)


def axial_rope(x, ang_row, ang_col):
    shp = (1, x.shape[1]) + (1,) * (x.ndim - 3) + (AXIAL_DIM // 2,)
    xr = rotate(x[..., :AXIAL_DIM], ang_row.reshape(shp))
    xc = rotate(x[..., AXIAL_DIM:], ang_col.reshape(shp))
    return jnp.concatenate([xr, xc], axis=-1)


def gqa_attention(q, k, v):
    b, s = q.shape[0], q.shape[1]
    nblk = s // Q_BLOCK
    scale = 1.0 / math.sqrt(HEAD_DIM)
    qb = jnp.moveaxis(q.reshape(b, nblk, Q_BLOCK, N_KV_HEADS, Q_PER_KV, HEAD_DIM), 1, 0)

    def block(q_blk):
        sc = jnp.einsum("bqkgd,bskd->bkgqs", q_blk, k).astype(jnp.float32) * scale
        p = jax.nn.softmax(sc, axis=-1).astype(v.dtype)
        return jnp.einsum("bkgqs,bskd->bqkgd", p, v)

    out = lax.map(block, qb)
    return jnp.moveaxis(out, 0, 1).reshape(b, s, ATTN_WIDTH)


def ssd_chunked(xs, dt, a, bm, cm):
    b, l, g, r, p = xs.shape
    n = bm.shape[-1]
    c = l // CHUNK
    xd = (xs * dt[..., None]).reshape(b, c, CHUNK, g, r, p)
    adt = jnp.moveaxis((dt * a).reshape(b, c, CHUNK, g, r), 2, -1)
    bc = bm.reshape(b, c, CHUNK, g, n)
    cc = cm.reshape(b, c, CHUNK, g, n)
    a_cs = jnp.cumsum(adt, axis=-1)
    tri = jnp.tril(jnp.ones((CHUNK, CHUNK), dtype=bool))
    seg = a_cs[..., :, None] - a_cs[..., None, :]
    decay_in = jnp.exp(jnp.where(tri, seg, -jnp.inf))
    cb = jnp.einsum("bclgn,bcsgn->bcgls", cc, bc)
    y_diag = jnp.einsum("bcgls,bcgrls,bcsgrp->bclgrp", cb, decay_in, xd)
    decay_states = jnp.exp(a_cs[..., -1:] - a_cs)
    states = jnp.einsum("bclgn,bcgrl,bclgrp->bcgrpn", bc, decay_states, xd)
    chunk_decay = jnp.exp(a_cs[..., -1])

    def step(h, inp):
        st, dec = inp
        return h * dec[..., None, None] + st, h

    h0 = jnp.zeros((b, g, r, p, n), dtype=xs.dtype)
    _, prev = lax.scan(step, h0, (jnp.moveaxis(states, 1, 0), jnp.moveaxis(chunk_decay, 1, 0)))
    prev = jnp.moveaxis(prev, 0, 1)
    y_off = jnp.einsum("bclgn,bcgrpn,bcgrl->bclgrp", cc, prev, jnp.exp(a_cs))
    return (y_diag + y_off).reshape(b, l, g, r, p)


def centred_depthwise_conv(u, w, bias):
    pad = (CONV_K - 1) // 2
    out = lax.conv_general_dilated(
        u, w[:, None, :].astype(u.dtype), window_strides=(1,), padding=[(pad, pad)],
        dimension_numbers=("NWC", "WIO", "NWC"), feature_group_count=u.shape[-1])
    return out + bias


def ssd_branch(z, xbc, dt_raw, conv_w, conv_b, dt_bias_f, dt_bias_b, a_log_f, a_log_b, d_skip, ssd_norm):
    b, l, _ = z.shape
    xbc = jax.nn.silu(centred_depthwise_conv(xbc, conv_w, conv_b))
    xs, bm, cm = split_cols(xbc, (SSD_D_INNER, SSD_GROUPS * SSD_STATE, SSD_GROUPS * SSD_STATE))
    f32 = jnp.float32
    xs = xs.astype(f32).reshape(b, l, SSD_GROUPS, SSD_HEADS_PER_GROUP, SSD_HEADDIM)
    bm = bm.astype(f32).reshape(b, l, SSD_GROUPS, SSD_STATE)
    cm = cm.astype(f32).reshape(b, l, SSD_GROUPS, SSD_STATE)
    dt_raw = dt_raw.astype(f32)
    shp = (SSD_GROUPS, SSD_HEADS_PER_GROUP)
    dt_f = jax.nn.softplus(dt_raw[..., :SSD_HEADS] + dt_bias_f.astype(f32)).reshape(b, l, *shp)
    dt_b = jax.nn.softplus(dt_raw[..., SSD_HEADS:] + dt_bias_b.astype(f32)).reshape(b, l, *shp)
    a_f = -jnp.exp(a_log_f.astype(f32)).reshape(shp)
    a_b = -jnp.exp(a_log_b.astype(f32)).reshape(shp)
    y_fwd = ssd_chunked(xs, dt_f, a_f, bm, cm)
    flip = lambda t: jnp.flip(t, axis=1)
    y_bwd = flip(ssd_chunked(flip(xs), flip(dt_b), a_b, flip(bm), flip(cm)))
    y = y_fwd + y_bwd + d_skip.astype(f32).reshape(shp)[..., None] * xs
    y = y.reshape(b, l, SSD_D_INNER).astype(z.dtype)
    return rms_norm(y * jax.nn.silu(z), ssd_norm)


def setup_inputs(seed: int = 0) -> dict:
    key = jax.random.key(seed)
    ks = jax.random.split(key, 24)
    L = DEPTH
    nrm = lambda k, shape, fan_in: jax.random.normal(k, shape, jnp.float32) * fan_in ** -0.5
    gain = lambda k, shape: 1.0 + 0.05 * jax.random.normal(k, shape, jnp.float32)
    dt0 = jnp.exp(jax.random.uniform(ks[6], (2, L, SSD_HEADS), jnp.float32, math.log(1e-3), math.log(1e-1)))
    dt_bias = dt0 + jnp.log(-jnp.expm1(-dt0))
    a_log = jnp.log(jax.random.uniform(ks[7], (2, L, SSD_HEADS), jnp.float32, 1.0, 16.0))
    return {
        "x": jax.random.normal(ks[0], (BATCH, SEQ, D_MODEL), jnp.float32),
        "w_in": nrm(ks[1], (L, D_MODEL, D_IN_PROJ), D_MODEL),
        "q_norm": gain(ks[2], (L, HEAD_DIM)),
        "k_norm": gain(ks[3], (L, HEAD_DIM)),
        "conv_w": nrm(ks[4], (L, CONV_K, CONV_CH), CONV_K),
        "conv_b": 0.02 * jax.random.normal(ks[5], (L, CONV_CH), jnp.float32),
        "dt_bias_f": dt_bias[0],
        "dt_bias_b": dt_bias[1],
        "a_log_f": a_log[0],
        "a_log_b": a_log[1],
        "d_skip": 1.0 + 0.1 * jax.random.normal(ks[8], (L, SSD_HEADS), jnp.float32),
        "ssd_norm": gain(ks[9], (L, SSD_D_INNER)),
        "w_attn_proj": nrm(ks[10], (L, ATTN_WIDTH, D_MODEL), ATTN_WIDTH),
        "w_ssd_proj": nrm(ks[11], (L, SSD_D_INNER, D_MODEL), SSD_D_INNER),
        "w_out": nrm(ks[12], (L, D_MODEL, D_MODEL), D_MODEL),
        "norm1_pre": gain(ks[13], (L, D_MODEL)),
        "norm1_post": gain(ks[14], (L, D_MODEL)),
        "norm2_pre": gain(ks[15], (L, D_MODEL)),
        "norm2_post": gain(ks[16], (L, D_MODEL)),
        "w_gate_up": nrm(ks[17], (L, D_MODEL, 2 * D_FF), D_MODEL),
        "w_down": nrm(ks[18], (L, D_FF, D_MODEL), D_FF),
    }


def reference(x, w_in, q_norm, k_norm, conv_w, conv_b, dt_bias_f, dt_bias_b, a_log_f, a_log_b,
              d_skip, ssd_norm, w_attn_proj, w_ssd_proj, w_out, norm1_pre, norm1_post,
              norm2_pre, norm2_post, w_gate_up, w_down):
    b, s, _ = x.shape
    ang_row, ang_col = axial_angles(s)
    for i in range(DEPTH):
        h = rms_norm(x, norm1_pre[i])
        proj = h @ w_in[i]
        q, k, v, z, xbc, dt_raw, gates = split_cols(proj, IN_SPLITS)
        q = rms_norm(q.reshape(b, s, N_KV_HEADS, Q_PER_KV, HEAD_DIM), q_norm[i])
        k = rms_norm(k.reshape(b, s, N_KV_HEADS, HEAD_DIM), k_norm[i])
        v = v.reshape(b, s, N_KV_HEADS, HEAD_DIM)
        q = axial_rope(q, ang_row, ang_col)
        k = axial_rope(k, ang_row, ang_col)
        attn_out = gqa_attention(q, k, v) @ w_attn_proj[i]
        ssd_y = ssd_branch(z, xbc, dt_raw, conv_w[i], conv_b[i], dt_bias_f[i], dt_bias_b[i],
                           a_log_f[i], a_log_b[i], d_skip[i], ssd_norm[i])
        ssd_out = ssd_y @ w_ssd_proj[i]
        g_attn, g_ssd = jnp.split(jax.nn.sigmoid(gates), N_BRANCH, axis=-1)
        mixed = (g_attn * attn_out + g_ssd * ssd_out) @ w_out[i]
        x = x + rms_norm(mixed, norm1_post[i])
        h2 = rms_norm(x, norm2_pre[i])
        gate, up = jnp.split(h2 @ w_gate_up[i], 2, axis=-1)
        ffn = (jax.nn.silu(gate) * up) @ w_down[i]
        x = x + rms_norm(ffn, norm2_post[i])
    return x
```

```python
import functools
import math

import jax
import jax.numpy as jnp
import numpy as np
from jax import lax
from jax.experimental import pallas as pl
from jax.experimental.pallas import tpu as pltpu

F32 = jnp.float32
BF16 = jnp.bfloat16

D_MODEL = 1024
GRID_W = 64
HEAD_DIM = 64
N_Q_HEADS = 8
N_KV_HEADS = 2
Q_PER_KV = N_Q_HEADS // N_KV_HEADS
ATTN_WIDTH = N_Q_HEADS * HEAD_DIM
KV_WIDTH = N_KV_HEADS * HEAD_DIM
QK_WIDTH = ATTN_WIDTH + KV_WIDTH
ROPE_THETA = 10000.0
AXIAL_DIM = HEAD_DIM // 2
SSD_D_INNER = D_MODEL
SSD_HEADDIM = 64
SSD_HEADS = SSD_D_INNER // SSD_HEADDIM
SSD_GROUPS = 2
HEADS_PER_GROUP = SSD_HEADS // SSD_GROUPS
GROUP_WIDTH = HEADS_PER_GROUP * SSD_HEADDIM
SSD_STATE = 128
CONV_K = 5
CONV_PAD = (CONV_K - 1) // 2
CHUNK = 128
CONV_CH = SSD_D_INNER + 2 * SSD_GROUPS * SSD_STATE
D_FF = ((8 * D_MODEL // 3 + 255) // 256) * 256
EPS = 1e-6
DT_PAD = 128
HALO = 16

VMEM_LIMIT = 56 * 1024 * 1024

TOK_BLOCK = 512
Q_BLOCK = 256
KV_BLOCK = 256


def _rms(xf, gain):
    return xf * lax.rsqrt(jnp.mean(xf * xf, axis=-1, keepdims=True) + EPS) * gain


def _split_bf16(a, parts):
    out = []
    r = a
    for _ in range(parts):
        p = r.astype(BF16)
        out.append(p)
        r = r - p.astype(F32)
    return out


def _dot(a, b):
    return jnp.dot(a, b, preferred_element_type=F32)


def _in_proj_kernel(x_ref, g_ref, wqkv_ref, wz_ref, wxbc_ref, wg_ref, wdt_ref, qkg_ref, cos_ref,
                    sin_ref, bd_ref, q_ref, k_ref, v_ref, z_ref, xbc_ref, gates_ref, dt_ref):
    h = _rms(x_ref[...], g_ref[...]).astype(BF16)
    z_ref[...] = _dot(h, wz_ref[...]).astype(BF16)
    xbc_ref[...] = _dot(h, wxbc_ref[...]).astype(BF16)
    gates_ref[...] = _dot(h, wg_ref[...]).astype(BF16)
    dt_ref[...] = _dot(h, wdt_ref[...])
    qkv = _dot(h, wqkv_ref[...])
    v_ref[...] = qkv[:, QK_WIDTH:].astype(BF16)

    qk = qkv[:, :QK_WIDTH]
    sq = qk * qk
    hi, lo = _split_bf16(sq, 2)
    bd = bd_ref[...]
    n_lane_tiles = QK_WIDTH // 128
    ms = jnp.concatenate(
        [_dot(hi[:, i * 128:(i + 1) * 128], bd) + _dot(lo[:, i * 128:(i + 1) * 128], bd)
         for i in range(n_lane_tiles)], axis=1) * (1.0 / HEAD_DIM)
    qn = qk * lax.rsqrt(ms + EPS) * qkg_ref[...]
    half = AXIAL_DIM // 2
    lane = lax.broadcasted_iota(jnp.int32, qn.shape, 1)
    partner = jnp.where((lane % AXIAL_DIM) < half,
                        pltpu.roll(qn, QK_WIDTH - half, 1), pltpu.roll(qn, half, 1))
    cos = jnp.concatenate([cos_ref[...]] * n_lane_tiles, axis=1)
    sin = jnp.concatenate([sin_ref[...]] * n_lane_tiles, axis=1)
    roped = qn * cos + partner * sin
    q_ref[...] = (roped[:, :ATTN_WIDTH] * (1.0 / math.sqrt(HEAD_DIM))).astype(BF16)
    k_ref[...] = roped[:, ATTN_WIDTH:].astype(BF16)


def _in_proj(x2, g, wqkv, wz, wxbc, wg, wdt, qkg, cos_t, sin_t, bd, seq):
    n_tok = x2.shape[0]
    tm = TOK_BLOCK
    blocks_per_seq = seq // tm
    full = lambda a: pl.BlockSpec(a.shape, lambda i: (0,) * a.ndim)
    row = lambda w: pl.BlockSpec((tm, w), lambda i: (i, 0))
    tab = pl.BlockSpec((tm, 128), lambda i: (i % blocks_per_seq, 0))
    out_shapes = (
        jax.ShapeDtypeStruct((n_tok, ATTN_WIDTH), BF16),
        jax.ShapeDtypeStruct((n_tok, KV_WIDTH), BF16),
        jax.ShapeDtypeStruct((n_tok, KV_WIDTH), BF16),
        jax.ShapeDtypeStruct((n_tok, SSD_D_INNER), BF16),
        jax.ShapeDtypeStruct((n_tok, CONV_CH), BF16),
        jax.ShapeDtypeStruct((n_tok, 2 * D_MODEL), BF16),
        jax.ShapeDtypeStruct((n_tok, DT_PAD), F32),
    )
    return pl.pallas_call(
        _in_proj_kernel,
        grid=(n_tok // tm,),
        in_specs=[row(D_MODEL), full(g), full(wqkv), full(wz), full(wxbc), full(wg), full(wdt),
                  full(qkg), tab, tab, full(bd)],
        out_specs=(row(ATTN_WIDTH), row(KV_WIDTH), row(KV_WIDTH), row(SSD_D_INNER), row(CONV_CH),
                   row(2 * D_MODEL), row(DT_PAD)),
        out_shape=out_shapes,
        compiler_params=pltpu.CompilerParams(dimension_semantics=("arbitrary",),
                                             vmem_limit_bytes=VMEM_LIMIT),
        name="in_proj",
    )(x2, g, wqkv, wz, wxbc, wg, wdt, qkg, cos_t, sin_t, bd)


def _attn_kernel(qt_ref, k_ref, vt_ref, o_ref, qpad_scr, m_scr, l_scr, acc_scr, *, n_kv_blocks):
    kvh = pl.program_id(1)
    bq = qt_ref.shape[1]
    q_tile = jnp.concatenate(
        [qt_ref[g * HEAD_DIM:(g + 1) * HEAD_DIM, :] for g in range(Q_PER_KV)], axis=1)
    zero = jnp.zeros_like(q_tile)
    qpad_scr[0:HEAD_DIM, :] = jnp.where(kvh == 0, q_tile, zero)
    qpad_scr[HEAD_DIM:, :] = jnp.where(kvh == 0, zero, q_tile)
    m_scr[...] = jnp.full(m_scr.shape, -1e30, F32)
    l_scr[...] = jnp.zeros(l_scr.shape, F32)
    acc_scr[...] = jnp.zeros(acc_scr.shape, F32)

    def body(j, carry):
        start = pl.multiple_of(j * KV_BLOCK, KV_BLOCK)
        s = _dot(k_ref[pl.ds(start, KV_BLOCK), :], qpad_scr[...])
        m_prev = m_scr[...]
        m_new = jnp.maximum(m_prev, jnp.max(s, axis=0, keepdims=True))
        alpha = jnp.exp(m_prev - m_new)
        p = jnp.exp(s - m_new)
        l_scr[...] = alpha * l_scr[...] + jnp.sum(p, axis=0, keepdims=True)
        pv = _dot(vt_ref[:, pl.ds(start, KV_BLOCK)], p.astype(BF16))
        acc_scr[...] = alpha * acc_scr[...] + pv
        m_scr[...] = m_new
        return carry

    lax.fori_loop(0, n_kv_blocks, body, 0)
    out = acc_scr[...] / l_scr[...]
    for g in range(Q_PER_KV):
        o_ref[g * HEAD_DIM:(g + 1) * HEAD_DIM, :] = out[:, g * bq:(g + 1) * bq].astype(BF16)


def _attention(qt, k, vt):
    b, _, s = qt.shape
    bq = Q_BLOCK
    rows = Q_PER_KV * HEAD_DIM
    return pl.pallas_call(
        functools.partial(_attn_kernel, n_kv_blocks=s // KV_BLOCK),
        grid=(b, N_KV_HEADS, s // bq),
        in_specs=[pl.BlockSpec((None, rows, bq), lambda bi, h, qi: (bi, h, qi)),
                  pl.BlockSpec((None, s, KV_WIDTH), lambda bi, h, qi: (bi, 0, 0)),
                  pl.BlockSpec((None, HEAD_DIM, s), lambda bi, h, qi: (bi, h, 0))],
        out_specs=pl.BlockSpec((None, rows, bq), lambda bi, h, qi: (bi, h, qi)),
        out_shape=jax.ShapeDtypeStruct((b, ATTN_WIDTH, s), BF16),
        scratch_shapes=[pltpu.VMEM((KV_WIDTH, Q_PER_KV * bq), BF16),
                        pltpu.VMEM((1, Q_PER_KV * bq), F32),
                        pltpu.VMEM((1, Q_PER_KV * bq), F32),
                        pltpu.VMEM((HEAD_DIM, Q_PER_KV * bq), F32)],
        compiler_params=pltpu.CompilerParams(
            dimension_semantics=("arbitrary", "arbitrary", "arbitrary"),
            vmem_limit_bytes=VMEM_LIMIT),
        name="attention",
    )(qt, k, vt)


def _softplus(x):
    return jnp.maximum(x, 0.0) + jnp.log1p(jnp.exp(-jnp.abs(x)))


def _ssd_direction(d, chunk, n_chunks, xc_ref, xp_ref, xn_ref, dt_ref, dtt_ref, convw_ref,
                   convb_ref, dtb_row_ref, dtb_col_ref, alog_row_ref, alog_col_ref, dskip_ref,
                   expand_ref, ext_scr, h_scr, y_ref):
    hs = slice(d * SSD_HEADS, (d + 1) * SSD_HEADS)
    keep_prev = (chunk > 0).astype(F32)
    keep_next = (chunk < n_chunks - 1).astype(F32)
    ext_scr[0:HALO, :] = xp_ref[...].astype(F32) * keep_prev
    ext_scr[HALO:HALO + CHUNK, :] = xc_ref[...].astype(F32)
    ext_scr[HALO + CHUNK:, :] = xn_ref[...].astype(F32) * keep_next
    conv = jnp.broadcast_to(convb_ref[...], (CHUNK, CONV_CH))
    for k in range(CONV_K):
        conv = conv + convw_ref[k:k + 1, :] * ext_scr[pl.ds(HALO - CONV_PAD + k, CHUNK), :]
    act = conv * jax.nn.sigmoid(conv)
    xs = act[:, :SSD_D_INNER]

    dt_c = _softplus(dt_ref[:, hs] + dtb_row_ref[:, hs])
    dt_r = _softplus(dtt_ref[hs, :] + dtb_col_ref[hs, :])
    adt_c = dt_c * (-jnp.exp(alog_row_ref[:, hs]))
    adt_r = dt_r * (-jnp.exp(alog_col_ref[hs, :]))
    ri = lax.broadcasted_iota(jnp.int32, (CHUNK, CHUNK), 0)
    ci = lax.broadcasted_iota(jnp.int32, (CHUNK, CHUNK), 1)
    lower = ri >= ci
    upper = ri <= ci
    before = upper if d else lower
    tri = jnp.where(before, 1.0, 0.0).astype(BF16)
    tri_t = jnp.where(lower if d else upper, 1.0, 0.0).astype(BF16)
    c_col = sum(_dot(tri, p) for p in _split_bf16(adt_c, 3))
    c_row = sum(_dot(p, tri_t) for p in _split_bf16(adt_r, 3))
    end = 0 if d else CHUNK - 1
    c_end = c_col[end:end + 1, :]
    expand = expand_ref[...]
    dt_x = _dot(dt_c.astype(BF16), expand)
    ec_x = _dot(jnp.exp(c_col).astype(BF16), expand)
    ed_x = _dot(jnp.exp(c_end - c_col).astype(BF16), expand)
    eend = jnp.broadcast_to(jnp.exp(c_end), (8, SSD_HEADS))
    eend_x = sum(_dot(p, expand) for p in _split_bf16(eend, 3))[0:1, :]

    xd = xs * dt_x
    xd_b = xd.astype(BF16)
    xde_b = (xd * ed_x).astype(BF16)
    y_groups = []
    for g in range(SSD_GROUPS):
        gs = slice(g * GROUP_WIDTH, (g + 1) * GROUP_WIDTH)
        b_g = act[:, SSD_D_INNER + g * SSD_STATE:SSD_D_INNER + (g + 1) * SSD_STATE].astype(BF16)
        c_off = SSD_D_INNER + SSD_GROUPS * SSD_STATE
        c_g = act[:, c_off + g * SSD_STATE:c_off + (g + 1) * SSD_STATE].astype(BF16)
        cb = lax.dot_general(c_g, b_g, (((1,), (1,)), ((), ())), preferred_element_type=F32)
        y_heads = []
        for r in range(HEADS_PER_GROUP):
            hd = g * HEADS_PER_GROUP + r
            seg = c_col[:, hd:hd + 1] - c_row[hd:hd + 1, :]
            w = (cb * jnp.where(before, jnp.exp(seg), 0.0)).astype(BF16)
            y_heads.append(_dot(w, xd_b[:, hd * SSD_HEADDIM:(hd + 1) * SSD_HEADDIM]))
        h_in = h_scr[g]
        y_off = _dot(c_g, h_in.astype(BF16)) * ec_x[:, gs]
        y_groups.append(jnp.concatenate(y_heads, axis=1) + y_off)
        upd = lax.dot_general(b_g, xde_b[:, gs], (((0,), (0,)), ((), ())),
                              preferred_element_type=F32)
        h_scr[g] = h_in * eend_x[:, gs] + upd
    y = jnp.concatenate(y_groups, axis=1)
    if d == 0:
        y = y + dskip_ref[...] * xs
    y_ref[...] = y.astype(BF16)


def _ssd_kernel(xcf, xpf, xnf, xcb, xpb, xnb, dtf, dttf, dtb, dttb, convw, convb, dtb_row,
                dtb_col, alog_row, alog_col, dskip, expand, yf_ref, yb_ref, ext_scr, hf_scr, hb_scr,
                *, n_chunks):
    i = pl.program_id(1)

    @pl.when(i == 0)
    def _():
        hf_scr[...] = jnp.zeros(hf_scr.shape, F32)
        hb_scr[...] = jnp.zeros(hb_scr.shape, F32)

    common = (convw, convb, dtb_row, dtb_col, alog_row, alog_col, dskip, expand, ext_scr)
    _ssd_direction(0, i, n_chunks, xcf, xpf, xnf, dtf, dttf, *common, hf_scr, yf_ref)
    _ssd_direction(1, n_chunks - 1 - i, n_chunks, xcb, xpb, xnb, dtb, dttb, *common, hb_scr, yb_ref)


def _ssd(xbc, dt, dtt, convw, convb, dtb_row, dtb_col, alog_row, alog_col, dskip, expand):
    b, s, _ = xbc.shape
    nc = s // CHUNK
    hpc = CHUNK // HALO
    last_halo = s // HALO - 1
    fwd = lambda bi, i: i
    bwd = lambda bi, i: nc - 1 - i

    def chunk_specs(cidx):
        cur = pl.BlockSpec((None, CHUNK, CONV_CH), lambda bi, i: (bi, cidx(bi, i), 0))
        prev = pl.BlockSpec((None, HALO, CONV_CH),
                            lambda bi, i: (bi, jnp.maximum(cidx(bi, i) * hpc - 1, 0), 0))
        nxt = pl.BlockSpec((None, HALO, CONV_CH),
                           lambda bi, i: (bi, jnp.minimum((cidx(bi, i) + 1) * hpc, last_halo), 0))
        return [cur, prev, nxt]

    def dt_specs(cidx):
        return [pl.BlockSpec((None, CHUNK, DT_PAD), lambda bi, i: (bi, cidx(bi, i), 0)),
                pl.BlockSpec((None, 2 * SSD_HEADS, CHUNK), lambda bi, i: (bi, 0, cidx(bi, i)))]

    full = lambda a: pl.BlockSpec(a.shape, lambda bi, i: (0,) * a.ndim)
    consts = (convw, convb, dtb_row, dtb_col, alog_row, alog_col, dskip, expand)
    y_spec = lambda cidx: pl.BlockSpec((None, CHUNK, SSD_D_INNER), lambda bi, i: (bi, cidx(bi, i), 0))
    y_shape = jax.ShapeDtypeStruct((b, s, SSD_D_INNER), BF16)
    return pl.pallas_call(
        functools.partial(_ssd_kernel, n_chunks=nc),
        grid=(b, nc),
        in_specs=chunk_specs(fwd) + chunk_specs(bwd) + dt_specs(fwd) + dt_specs(bwd)
        + [full(a) for a in consts],
        out_specs=(y_spec(fwd), y_spec(bwd)),
        out_shape=(y_shape, y_shape),
        scratch_shapes=[pltpu.VMEM((CHUNK + 2 * HALO, CONV_CH), F32),
                        pltpu.VMEM((SSD_GROUPS, SSD_STATE, GROUP_WIDTH), F32),
                        pltpu.VMEM((SSD_GROUPS, SSD_STATE, GROUP_WIDTH), F32)],
        compiler_params=pltpu.CompilerParams(dimension_semantics=("arbitrary", "arbitrary"),
                                             vmem_limit_bytes=VMEM_LIMIT),
        name="ssd",
    )(xbc, xbc, xbc, xbc, xbc, xbc, dt, dtt, dt, dtt, *consts)


def _merge_kernel(x_ref, yf_ref, yb_ref, z_ref, ot_ref, gates_ref, wssd_ref, wattn_ref, wout_ref,
                  gssd_ref, gpost_ref, o_ref):
    y = yf_ref[...].astype(F32) + yb_ref[...].astype(F32)
    z = z_ref[...].astype(F32)
    ssd_in = _rms(y * (z * jax.nn.sigmoid(z)), gssd_ref[...]).astype(BF16)
    ssd_out = _dot(ssd_in, wssd_ref[...])
    attn_out = lax.dot_general(ot_ref[...], wattn_ref[...], (((0,), (0,)), ((), ())),
                               preferred_element_type=F32)
    gates = jax.nn.sigmoid(gates_ref[...].astype(F32))
    mixed = gates[:, :D_MODEL] * attn_out + gates[:, D_MODEL:] * ssd_out
    proj = _dot(mixed.astype(BF16), wout_ref[...])
    o_ref[...] = x_ref[...] + _rms(proj, gpost_ref[...])


def _merge(x2, yf, yb, z, ot, gates, wssd, wattn, wout, gssd, gpost, seq):
    n_tok = x2.shape[0]
    tm = TOK_BLOCK
    bps = seq // tm
    full = lambda a: pl.BlockSpec(a.shape, lambda i: (0,) * a.ndim)
    row = lambda w: pl.BlockSpec((tm, w), lambda i: (i, 0))
    return pl.pallas_call(
        _merge_kernel,
        grid=(n_tok // tm,),
        in_specs=[row(D_MODEL), row(SSD_D_INNER), row(SSD_D_INNER), row(SSD_D_INNER),
                  pl.BlockSpec((None, ATTN_WIDTH, tm), lambda i: (i // bps, 0, i % bps)),
                  row(2 * D_MODEL), full(wssd), full(wattn), full(wout), full(gssd), full(gpost)],
        out_specs=row(D_MODEL),
        out_shape=jax.ShapeDtypeStruct((n_tok, D_MODEL), F32),
        compiler_params=pltpu.CompilerParams(dimension_semantics=("arbitrary",),
                                             vmem_limit_bytes=VMEM_LIMIT),
        name="merge",
    )(x2, yf, yb, z, ot, gates, wssd, wattn, wout, gssd, gpost)


def _ffn_kernel(x_ref, gpre_ref, wgate_ref, wup_ref, wdown_ref, gpost_ref, o_ref):
    x = x_ref[...]
    h = _rms(x, gpre_ref[...]).astype(BF16)
    gate = _dot(h, wgate_ref[...])
    up = _dot(h, wup_ref[...])
    act = (gate * jax.nn.sigmoid(gate) * up).astype(BF16)
    ffn = _dot(act, wdown_ref[...])
    o_ref[...] = x + _rms(ffn, gpost_ref[...])


def _ffn(x2, gpre, wgate, wup, wdown, gpost):
    n_tok = x2.shape[0]
    tm = TOK_BLOCK
    full = lambda a: pl.BlockSpec(a.shape, lambda i: (0,) * a.ndim)
    row = pl.BlockSpec((tm, D_MODEL), lambda i: (i, 0))
    return pl.pallas_call(
        _ffn_kernel,
        grid=(n_tok // tm,),
        in_specs=[row, full(gpre), full(wgate), full(wup), full(wdown), full(gpost)],
        out_specs=row,
        out_shape=jax.ShapeDtypeStruct((n_tok, D_MODEL), F32),
        compiler_params=pltpu.CompilerParams(dimension_semantics=("arbitrary",),
                                             vmem_limit_bytes=VMEM_LIMIT),
        name="ffn",
    )(x2, gpre, wgate, wup, wdown, gpost)


def _rope_tables(seq):
    rows = seq // GRID_W
    row_idx = jnp.repeat(jnp.arange(rows, dtype=F32), GRID_W)
    col_idx = jnp.tile(jnp.arange(GRID_W, dtype=F32), rows)
    inv_freq = ROPE_THETA ** (-jnp.arange(0, AXIAL_DIM, 2, dtype=F32) / AXIAL_DIM)
    ang_row = row_idx[:, None] * inv_freq[None, :]
    ang_col = col_idx[:, None] * inv_freq[None, :]
    cr, sr, cc, sc = jnp.cos(ang_row), jnp.sin(ang_row), jnp.cos(ang_col), jnp.sin(ang_col)
    cos64 = jnp.concatenate([cr, cr, cc, cc], axis=-1)
    sin64 = jnp.concatenate([-sr, sr, -sc, sc], axis=-1)
    return jnp.tile(cos64, (1, 2)), jnp.tile(sin64, (1, 2))


def kernel(x, w_in, q_norm, k_norm, conv_w, conv_b, dt_bias_f, dt_bias_b, a_log_f, a_log_b, d_skip, ssd_norm, w_attn_proj, w_ssd_proj, w_out, norm1_pre, norm1_post, norm2_pre, norm2_post, w_gate_up, w_down):
    b, s, _ = x.shape
    assert w_in.shape[0] == 1, "single-layer block"
    assert s % TOK_BLOCK == 0 and s % Q_BLOCK == 0 and s % KV_BLOCK == 0 and s % CHUNK == 0
    x2 = x.reshape(b * s, D_MODEL)

    w = w_in[0]
    o_z = QK_WIDTH + KV_WIDTH
    o_xbc = o_z + SSD_D_INNER
    o_dt = o_xbc + CONV_CH
    o_g = o_dt + 2 * SSD_HEADS
    wqkv = w[:, :o_z].astype(BF16)
    wz = w[:, o_z:o_xbc].astype(BF16)
    wxbc = w[:, o_xbc:o_dt].astype(BF16)
    wdt = jnp.pad(w[:, o_dt:o_g], ((0, 0), (0, DT_PAD - 2 * SSD_HEADS))).astype(BF16)
    wg = w[:, o_g:].astype(BF16)
    row = lambda v: v.reshape(1, -1).astype(F32)
    qkg = jnp.concatenate([jnp.tile(q_norm[0], N_Q_HEADS), jnp.tile(k_norm[0], N_KV_HEADS)]).reshape(1, -1)
    cos_t, sin_t = _rope_tables(s)
    seg = np.arange(128) // HEAD_DIM
    bd = jnp.asarray(seg[:, None] == seg[None, :], dtype=BF16)

    q, k, v, z, xbc, gates, dt = _in_proj(x2, row(norm1_pre[0]), wqkv, wz, wxbc, wg, wdt, qkg,
                                          cos_t, sin_t, bd, s)

    qt = jnp.swapaxes(q.reshape(b, s, ATTN_WIDTH), 1, 2)
    vt = jnp.swapaxes(v.reshape(b, s, KV_WIDTH), 1, 2)
    ot = _attention(qt, k.reshape(b, s, KV_WIDTH), vt)

    dt3 = dt.reshape(b, s, DT_PAD)
    dtt = jnp.swapaxes(dt3[:, :, :2 * SSD_HEADS], 1, 2)
    dt_bias = jnp.concatenate([dt_bias_f[0], dt_bias_b[0]]).astype(F32)
    a_log = jnp.concatenate([a_log_f[0], a_log_b[0]]).astype(F32)
    convw = jnp.pad(conv_w[0].astype(F32), ((0, 8 - CONV_K), (0, 0)))
    dskip = jnp.repeat(d_skip[0].astype(F32), SSD_HEADDIM).reshape(1, -1)
    hid = np.arange(SSD_D_INNER) // SSD_HEADDIM
    expand = jnp.asarray(np.arange(SSD_HEADS)[:, None] == hid[None, :], dtype=BF16)
    yf, yb = _ssd(xbc.reshape(b, s, CONV_CH), dt3, dtt, convw, row(conv_b[0]),
                  dt_bias.reshape(1, -1), dt_bias.reshape(-1, 1), a_log.reshape(1, -1),
                  a_log.reshape(-1, 1), dskip, expand)

    x1 = _merge(x2, yf.reshape(b * s, -1), yb.reshape(b * s, -1), z, ot, gates,
                w_ssd_proj[0].astype(BF16), w_attn_proj[0].astype(BF16), w_out[0].astype(BF16),
                row(ssd_norm[0]), row(norm1_post[0]), s)

    wgu = w_gate_up[0]
    out = _ffn(x1, row(norm2_pre[0]), wgu[:, :D_FF].astype(BF16), wgu[:, D_FF:].astype(BF16),
               w_down[0].astype(BF16), row(norm2_post[0]))
    return out.reshape(b, s, D_MODEL)
```

```python
import functools
import math

import jax
import jax.numpy as jnp
import numpy as np
from jax import lax
from jax.experimental import pallas as pl
from jax.experimental.pallas import tpu as pltpu

F32 = jnp.float32
BF16 = jnp.bfloat16

D_MODEL = 1024
GRID_W = 64
HEAD_DIM = 64
N_Q_HEADS = 8
N_KV_HEADS = 2
Q_PER_KV = N_Q_HEADS // N_KV_HEADS
ATTN_WIDTH = N_Q_HEADS * HEAD_DIM
KV_WIDTH = N_KV_HEADS * HEAD_DIM
QK_WIDTH = ATTN_WIDTH + KV_WIDTH
ROPE_THETA = 10000.0
AXIAL_DIM = HEAD_DIM // 2
SSD_D_INNER = D_MODEL
SSD_HEADDIM = 64
SSD_HEADS = SSD_D_INNER // SSD_HEADDIM
SSD_GROUPS = 2
HEADS_PER_GROUP = SSD_HEADS // SSD_GROUPS
GROUP_WIDTH = HEADS_PER_GROUP * SSD_HEADDIM
SSD_STATE = 128
CONV_K = 5
CONV_PAD = (CONV_K - 1) // 2
CHUNK = 128
CONV_CH = SSD_D_INNER + 2 * SSD_GROUPS * SSD_STATE
D_FF = ((8 * D_MODEL // 3 + 255) // 256) * 256
EPS = 1e-6
DT_PAD = 128
HALO = 16

VMEM_LIMIT = 56 * 1024 * 1024

TOK_BLOCK = 512
Q_BLOCK = 512
KV_BLOCK = 256
Q_TILE = 512
KV_UNROLL = 8
ONES_ROWS = 16
V_ROWS = HEAD_DIM + ONES_ROWS


def _rms(xf, gain):
    return xf * lax.rsqrt(jnp.mean(xf * xf, axis=-1, keepdims=True) + EPS) * gain


def _split_bf16(a, parts):
    out = []
    r = a
    for _ in range(parts):
        p = r.astype(BF16)
        out.append(p)
        r = r - p.astype(F32)
    return out


def _dot(a, b):
    return jnp.dot(a, b, preferred_element_type=F32)


def _in_proj_kernel(x_ref, g_ref, wqkv_ref, wz_ref, wxbc_ref, wg_ref, wdt_ref, qkg_ref, cos_ref,
                    sin_ref, bd_ref, q_ref, k_ref, v_ref, z_ref, xbc_ref, gates_ref, dt_ref):
    h = _rms(x_ref[...], g_ref[...]).astype(BF16)
    z_ref[...] = _dot(h, wz_ref[...]).astype(BF16)
    xbc_ref[...] = _dot(h, wxbc_ref[...]).astype(BF16)
    gates_ref[...] = _dot(h, wg_ref[...]).astype(BF16)
    dt_ref[...] = _dot(h, wdt_ref[...])
    qkv = _dot(h, wqkv_ref[...])
    v_ref[...] = qkv[:, QK_WIDTH:].astype(BF16)

    qk = qkv[:, :QK_WIDTH]
    sq = qk * qk
    hi, lo = _split_bf16(sq, 2)
    bd = bd_ref[...]
    n_lane_tiles = QK_WIDTH // 128
    ms = jnp.concatenate(
        [_dot(hi[:, i * 128:(i + 1) * 128], bd) + _dot(lo[:, i * 128:(i + 1) * 128], bd)
         for i in range(n_lane_tiles)], axis=1) * (1.0 / HEAD_DIM)
    qn = qk * lax.rsqrt(ms + EPS) * qkg_ref[...]
    half = AXIAL_DIM // 2
    lane = lax.broadcasted_iota(jnp.int32, qn.shape, 1)
    partner = jnp.where((lane % AXIAL_DIM) < half,
                        pltpu.roll(qn, QK_WIDTH - half, 1), pltpu.roll(qn, half, 1))
    cos = jnp.concatenate([cos_ref[...]] * n_lane_tiles, axis=1)
    sin = jnp.concatenate([sin_ref[...]] * n_lane_tiles, axis=1)
    roped = qn * cos + partner * sin
    q_ref[...] = (roped[:, :ATTN_WIDTH] * (math.log2(math.e) / math.sqrt(HEAD_DIM))).astype(BF16)
    k_ref[...] = roped[:, ATTN_WIDTH:].astype(BF16)


def _in_proj(x2, g, wqkv, wz, wxbc, wg, wdt, qkg, cos_t, sin_t, bd, seq):
    n_tok = x2.shape[0]
    tm = TOK_BLOCK
    blocks_per_seq = seq // tm
    full = lambda a: pl.BlockSpec(a.shape, lambda i: (0,) * a.ndim)
    row = lambda w: pl.BlockSpec((tm, w), lambda i: (i, 0))
    tab = pl.BlockSpec((tm, 128), lambda i: (i % blocks_per_seq, 0))
    out_shapes = (
        jax.ShapeDtypeStruct((n_tok, ATTN_WIDTH), BF16),
        jax.ShapeDtypeStruct((n_tok, KV_WIDTH), BF16),
        jax.ShapeDtypeStruct((n_tok, KV_WIDTH), BF16),
        jax.ShapeDtypeStruct((n_tok, SSD_D_INNER), BF16),
        jax.ShapeDtypeStruct((n_tok, CONV_CH), BF16),
        jax.ShapeDtypeStruct((n_tok, 2 * D_MODEL), BF16),
        jax.ShapeDtypeStruct((n_tok, DT_PAD), F32),
    )
    return pl.pallas_call(
        _in_proj_kernel,
        grid=(n_tok // tm,),
        in_specs=[row(D_MODEL), full(g), full(wqkv), full(wz), full(wxbc), full(wg), full(wdt),
                  full(qkg), tab, tab, full(bd)],
        out_specs=(row(ATTN_WIDTH), row(KV_WIDTH), row(KV_WIDTH), row(SSD_D_INNER), row(CONV_CH),
                   row(2 * D_MODEL), row(DT_PAD)),
        out_shape=out_shapes,
        compiler_params=pltpu.CompilerParams(dimension_semantics=("arbitrary",),
                                             vmem_limit_bytes=VMEM_LIMIT),
        name="in_proj",
    )(x2, g, wqkv, wz, wxbc, wg, wdt, qkg, cos_t, sin_t, bd)


def _sublane_allreduce(x, op):
    for shift in (4, 2, 1):
        x = op(x, pltpu.roll(x, shift, 0))
    return x


def _attn_kernel(qt_ref, k_ref, vt_ref, o_ref, qpad_scr, m_scr, acc_scr, s_scr, cmax_scr, *,
                 n_kv_blocks):
    kvh = pl.program_id(1)
    bq = qt_ref.shape[1]
    n_tiles = (Q_PER_KV * bq) // Q_TILE
    q_tile = jnp.concatenate(
        [qt_ref[g * HEAD_DIM:(g + 1) * HEAD_DIM, :] for g in range(Q_PER_KV)], axis=1)
    zero = jnp.zeros_like(q_tile)
    qpad_scr[0:HEAD_DIM, :] = jnp.where(kvh == 0, q_tile, zero)
    qpad_scr[HEAD_DIM:, :] = jnp.where(kvh == 0, zero, q_tile)
    m_scr[...] = jnp.full(m_scr.shape, -1e30, F32)
    acc_scr[...] = jnp.zeros(acc_scr.shape, F32)

    def scores(start, t):
        return _dot(k_ref[pl.ds(start, KV_BLOCK), :], qpad_scr[:, t * Q_TILE:(t + 1) * Q_TILE])

    def colmax(s):
        return _sublane_allreduce(jnp.max(s.reshape(KV_BLOCK // 8, 8, Q_TILE), axis=0), jnp.maximum)

    def softmax_pv(t, s, cmax, vt_blk):
        cols = slice(t * Q_TILE, (t + 1) * Q_TILE)
        m_prev = m_scr[:, cols]
        m_new = jnp.maximum(m_prev, cmax)
        alpha = jnp.exp2(m_prev - m_new)
        p = jnp.exp2(s.reshape(KV_BLOCK // 8, 8, Q_TILE) - m_new[None])
        pv = _dot(vt_blk, p.reshape(KV_BLOCK, Q_TILE).astype(BF16))
        acc = acc_scr[:, cols].reshape(V_ROWS // 8, 8, Q_TILE) * alpha[None]
        acc_scr[:, cols] = acc.reshape(V_ROWS, Q_TILE) + pv
        m_scr[:, cols] = m_new

    assert n_tiles >= 2
    s_scr[0] = scores(0, 0)
    s_scr[1] = scores(0, 1)
    cmax_scr[...] = colmax(s_scr[0])

    def body(j, carry):
        start = pl.multiple_of(j * KV_BLOCK, KV_BLOCK)
        nxt = pl.multiple_of(jnp.minimum(j + 1, n_kv_blocks - 1) * KV_BLOCK, KV_BLOCK)
        vt_blk = vt_ref[:, pl.ds(start, KV_BLOCK)]
        s_cur, s_nxt, cmax_cur = s_scr[0], s_scr[1], cmax_scr[...]
        for t in range(n_tiles):
            ahead = t + 2
            s_new = scores(start, ahead) if ahead < n_tiles else scores(nxt, ahead - n_tiles)
            cmax_nxt = colmax(s_nxt)
            softmax_pv(t, s_cur, cmax_cur, vt_blk)
            s_cur, s_nxt, cmax_cur = s_nxt, s_new, cmax_nxt
        s_scr[0] = s_cur
        s_scr[1] = s_nxt
        cmax_scr[...] = cmax_cur
        return carry

    lax.fori_loop(0, n_kv_blocks, body, 0, unroll=KV_UNROLL)
    inv_l = 1.0 / acc_scr[HEAD_DIM:HEAD_DIM + 8, :]
    out = acc_scr[0:HEAD_DIM, :].reshape(HEAD_DIM // 8, 8, Q_PER_KV * bq) * inv_l[None]
    out = out.reshape(HEAD_DIM, Q_PER_KV * bq)
    for g in range(Q_PER_KV):
        o_ref[g * HEAD_DIM:(g + 1) * HEAD_DIM, :] = out[:, g * bq:(g + 1) * bq].astype(BF16)


def _attention(qt, k, vt):
    b, _, s = qt.shape
    bq = Q_BLOCK
    rows = Q_PER_KV * HEAD_DIM
    return pl.pallas_call(
        functools.partial(_attn_kernel, n_kv_blocks=s // KV_BLOCK),
        grid=(b, N_KV_HEADS, s // bq),
        in_specs=[pl.BlockSpec((None, rows, bq), lambda bi, h, qi: (bi, h, qi)),
                  pl.BlockSpec((None, s, KV_WIDTH), lambda bi, h, qi: (bi, 0, 0)),
                  pl.BlockSpec((None, V_ROWS, s), lambda bi, h, qi: (bi, h, 0))],
        out_specs=pl.BlockSpec((None, rows, bq), lambda bi, h, qi: (bi, h, qi)),
        out_shape=jax.ShapeDtypeStruct((b, ATTN_WIDTH, s), BF16),
        scratch_shapes=[pltpu.VMEM((KV_WIDTH, Q_PER_KV * bq), BF16),
                        pltpu.VMEM((8, Q_PER_KV * bq), F32),
                        pltpu.VMEM((V_ROWS, Q_PER_KV * bq), F32),
                        pltpu.VMEM((2, KV_BLOCK, Q_TILE), F32),
                        pltpu.VMEM((8, Q_TILE), F32)],
        compiler_params=pltpu.CompilerParams(
            dimension_semantics=("arbitrary", "arbitrary", "arbitrary"),
            vmem_limit_bytes=VMEM_LIMIT),
        name="attention",
    )(qt, k, vt)


def _softplus(x):
    return jnp.maximum(x, 0.0) + jnp.log1p(jnp.exp(-jnp.abs(x)))


def _ssd_direction(d, chunk, n_chunks, xc_ref, xp_ref, xn_ref, dt_ref, dtt_ref, convw_ref,
                   convb_ref, dtb_row_ref, dtb_col_ref, alog_row_ref, alog_col_ref, dskip_ref,
                   expand_ref, ext_scr, h_scr, y_ref):
    hs = slice(d * SSD_HEADS, (d + 1) * SSD_HEADS)
    keep_prev = (chunk > 0).astype(F32)
    keep_next = (chunk < n_chunks - 1).astype(F32)
    ext_scr[0:HALO, :] = xp_ref[...].astype(F32) * keep_prev
    ext_scr[HALO:HALO + CHUNK, :] = xc_ref[...].astype(F32)
    ext_scr[HALO + CHUNK:, :] = xn_ref[...].astype(F32) * keep_next
    conv = jnp.broadcast_to(convb_ref[...], (CHUNK, CONV_CH))
    for k in range(CONV_K):
        conv = conv + convw_ref[k:k + 1, :] * ext_scr[pl.ds(HALO - CONV_PAD + k, CHUNK), :]
    act = conv * jax.nn.sigmoid(conv)
    xs = act[:, :SSD_D_INNER]

    dt_c = _softplus(dt_ref[:, hs] + dtb_row_ref[:, hs])
    dt_r = _softplus(dtt_ref[hs, :] + dtb_col_ref[hs, :])
    adt_c = dt_c * (-jnp.exp(alog_row_ref[:, hs]))
    adt_r = dt_r * (-jnp.exp(alog_col_ref[hs, :]))
    ri = lax.broadcasted_iota(jnp.int32, (CHUNK, CHUNK), 0)
    ci = lax.broadcasted_iota(jnp.int32, (CHUNK, CHUNK), 1)
    lower = ri >= ci
    upper = ri <= ci
    before = upper if d else lower
    tri = jnp.where(before, 1.0, 0.0).astype(BF16)
    tri_t = jnp.where(lower if d else upper, 1.0, 0.0).astype(BF16)
    c_col = sum(_dot(tri, p) for p in _split_bf16(adt_c, 3))
    c_row = sum(_dot(p, tri_t) for p in _split_bf16(adt_r, 3))
    end = 0 if d else CHUNK - 1
    c_end = c_col[end:end + 1, :]
    expand = expand_ref[...]
    dt_x = _dot(dt_c.astype(BF16), expand)
    ec_x = _dot(jnp.exp(c_col).astype(BF16), expand)
    ed_x = _dot(jnp.exp(c_end - c_col).astype(BF16), expand)
    eend = jnp.broadcast_to(jnp.exp(c_end), (8, SSD_HEADS))
    eend_x = sum(_dot(p, expand) for p in _split_bf16(eend, 3))[0:1, :]

    xd = xs * dt_x
    xd_b = xd.astype(BF16)
    xde_b = (xd * ed_x).astype(BF16)
    y_groups = []
    for g in range(SSD_GROUPS):
        gs = slice(g * GROUP_WIDTH, (g + 1) * GROUP_WIDTH)
        b_g = act[:, SSD_D_INNER + g * SSD_STATE:SSD_D_INNER + (g + 1) * SSD_STATE].astype(BF16)
        c_off = SSD_D_INNER + SSD_GROUPS * SSD_STATE
        c_g = act[:, c_off + g * SSD_STATE:c_off + (g + 1) * SSD_STATE].astype(BF16)
        cb = lax.dot_general(c_g, b_g, (((1,), (1,)), ((), ())), preferred_element_type=F32)
        y_heads = []
        for r in range(HEADS_PER_GROUP):
            hd = g * HEADS_PER_GROUP + r
            seg = c_col[:, hd:hd + 1] - c_row[hd:hd + 1, :]
            w = (cb * jnp.where(before, jnp.exp(seg), 0.0)).astype(BF16)
            y_heads.append(_dot(w, xd_b[:, hd * SSD_HEADDIM:(hd + 1) * SSD_HEADDIM]))
        h_in = h_scr[g]
        y_off = _dot(c_g, h_in.astype(BF16)) * ec_x[:, gs]
        y_groups.append(jnp.concatenate(y_heads, axis=1) + y_off)
        upd = lax.dot_general(b_g, xde_b[:, gs], (((0,), (0,)), ((), ())),
                              preferred_element_type=F32)
        h_scr[g] = h_in * eend_x[:, gs] + upd
    y = jnp.concatenate(y_groups, axis=1)
    if d == 0:
        y = y + dskip_ref[...] * xs
    y_ref[...] = y.astype(BF16)


def _ssd_kernel(xcf, xpf, xnf, xcb, xpb, xnb, dtf, dttf, dtb, dttb, convw, convb, dtb_row,
                dtb_col, alog_row, alog_col, dskip, expand, yf_ref, yb_ref, ext_scr, hf_scr, hb_scr,
                *, n_chunks):
    i = pl.program_id(1)

    @pl.when(i == 0)
    def _():
        hf_scr[...] = jnp.zeros(hf_scr.shape, F32)
        hb_scr[...] = jnp.zeros(hb_scr.shape, F32)

    common = (convw, convb, dtb_row, dtb_col, alog_row, alog_col, dskip, expand, ext_scr)
    _ssd_direction(0, i, n_chunks, xcf, xpf, xnf, dtf, dttf, *common, hf_scr, yf_ref)
    _ssd_direction(1, n_chunks - 1 - i, n_chunks, xcb, xpb, xnb, dtb, dttb, *common, hb_scr, yb_ref)


def _ssd(xbc, dt, dtt, convw, convb, dtb_row, dtb_col, alog_row, alog_col, dskip, expand):
    b, s, _ = xbc.shape
    nc = s // CHUNK
    hpc = CHUNK // HALO
    last_halo = s // HALO - 1
    fwd = lambda bi, i: i
    bwd = lambda bi, i: nc - 1 - i

    def chunk_specs(cidx):
        cur = pl.BlockSpec((None, CHUNK, CONV_CH), lambda bi, i: (bi, cidx(bi, i), 0))
        prev = pl.BlockSpec((None, HALO, CONV_CH),
                            lambda bi, i: (bi, jnp.maximum(cidx(bi, i) * hpc - 1, 0), 0))
        nxt = pl.BlockSpec((None, HALO, CONV_CH),
                           lambda bi, i: (bi, jnp.minimum((cidx(bi, i) + 1) * hpc, last_halo), 0))
        return [cur, prev, nxt]

    def dt_specs(cidx):
        return [pl.BlockSpec((None, CHUNK, DT_PAD), lambda bi, i: (bi, cidx(bi, i), 0)),
                pl.BlockSpec((None, 2 * SSD_HEADS, CHUNK), lambda bi, i: (bi, 0, cidx(bi, i)))]

    full = lambda a: pl.BlockSpec(a.shape, lambda bi, i: (0,) * a.ndim)
    consts = (convw, convb, dtb_row, dtb_col, alog_row, alog_col, dskip, expand)
    y_spec = lambda cidx: pl.BlockSpec((None, CHUNK, SSD_D_INNER), lambda bi, i: (bi, cidx(bi, i), 0))
    y_shape = jax.ShapeDtypeStruct((b, s, SSD_D_INNER), BF16)
    return pl.pallas_call(
        functools.partial(_ssd_kernel, n_chunks=nc),
        grid=(b, nc),
        in_specs=chunk_specs(fwd) + chunk_specs(bwd) + dt_specs(fwd) + dt_specs(bwd)
        + [full(a) for a in consts],
        out_specs=(y_spec(fwd), y_spec(bwd)),
        out_shape=(y_shape, y_shape),
        scratch_shapes=[pltpu.VMEM((CHUNK + 2 * HALO, CONV_CH), F32),
                        pltpu.VMEM((SSD_GROUPS, SSD_STATE, GROUP_WIDTH), F32),
                        pltpu.VMEM((SSD_GROUPS, SSD_STATE, GROUP_WIDTH), F32)],
        compiler_params=pltpu.CompilerParams(dimension_semantics=("arbitrary", "arbitrary"),
                                             vmem_limit_bytes=VMEM_LIMIT),
        name="ssd",
    )(xbc, xbc, xbc, xbc, xbc, xbc, dt, dtt, dt, dtt, *consts)


def _merge_kernel(x_ref, yf_ref, yb_ref, z_ref, ot_ref, gates_ref, wssd_ref, wattn_ref, wout_ref,
                  gssd_ref, gpost_ref, o_ref):
    y = yf_ref[...].astype(F32) + yb_ref[...].astype(F32)
    z = z_ref[...].astype(F32)
    ssd_in = _rms(y * (z * jax.nn.sigmoid(z)), gssd_ref[...]).astype(BF16)
    ssd_out = _dot(ssd_in, wssd_ref[...])
    attn_out = lax.dot_general(ot_ref[...], wattn_ref[...], (((0,), (0,)), ((), ())),
                               preferred_element_type=F32)
    gates = jax.nn.sigmoid(gates_ref[...].astype(F32))
    mixed = gates[:, :D_MODEL] * attn_out + gates[:, D_MODEL:] * ssd_out
    proj = _dot(mixed.astype(BF16), wout_ref[...])
    o_ref[...] = x_ref[...] + _rms(proj, gpost_ref[...])


def _merge(x2, yf, yb, z, ot, gates, wssd, wattn, wout, gssd, gpost, seq):
    n_tok = x2.shape[0]
    tm = TOK_BLOCK
    bps = seq // tm
    full = lambda a: pl.BlockSpec(a.shape, lambda i: (0,) * a.ndim)
    row = lambda w: pl.BlockSpec((tm, w), lambda i: (i, 0))
    return pl.pallas_call(
        _merge_kernel,
        grid=(n_tok // tm,),
        in_specs=[row(D_MODEL), row(SSD_D_INNER), row(SSD_D_INNER), row(SSD_D_INNER),
                  pl.BlockSpec((None, ATTN_WIDTH, tm), lambda i: (i // bps, 0, i % bps)),
                  row(2 * D_MODEL), full(wssd), full(wattn), full(wout), full(gssd), full(gpost)],
        out_specs=row(D_MODEL),
        out_shape=jax.ShapeDtypeStruct((n_tok, D_MODEL), F32),
        compiler_params=pltpu.CompilerParams(dimension_semantics=("arbitrary",),
                                             vmem_limit_bytes=VMEM_LIMIT),
        name="merge",
    )(x2, yf, yb, z, ot, gates, wssd, wattn, wout, gssd, gpost)


def _ffn_kernel(x_ref, gpre_ref, wgate_ref, wup_ref, wdown_ref, gpost_ref, o_ref):
    x = x_ref[...]
    h = _rms(x, gpre_ref[...]).astype(BF16)
    gate = _dot(h, wgate_ref[...])
    up = _dot(h, wup_ref[...])
    act = (gate * jax.nn.sigmoid(gate) * up).astype(BF16)
    ffn = _dot(act, wdown_ref[...])
    o_ref[...] = x + _rms(ffn, gpost_ref[...])


def _ffn(x2, gpre, wgate, wup, wdown, gpost):
    n_tok = x2.shape[0]
    tm = TOK_BLOCK
    full = lambda a: pl.BlockSpec(a.shape, lambda i: (0,) * a.ndim)
    row = pl.BlockSpec((tm, D_MODEL), lambda i: (i, 0))
    return pl.pallas_call(
        _ffn_kernel,
        grid=(n_tok // tm,),
        in_specs=[row, full(gpre), full(wgate), full(wup), full(wdown), full(gpost)],
        out_specs=row,
        out_shape=jax.ShapeDtypeStruct((n_tok, D_MODEL), F32),
        compiler_params=pltpu.CompilerParams(dimension_semantics=("arbitrary",),
                                             vmem_limit_bytes=VMEM_LIMIT),
        name="ffn",
    )(x2, gpre, wgate, wup, wdown, gpost)


def _rope_tables(seq):
    rows = seq // GRID_W
    row_idx = jnp.repeat(jnp.arange(rows, dtype=F32), GRID_W)
    col_idx = jnp.tile(jnp.arange(GRID_W, dtype=F32), rows)
    inv_freq = ROPE_THETA ** (-jnp.arange(0, AXIAL_DIM, 2, dtype=F32) / AXIAL_DIM)
    ang_row = row_idx[:, None] * inv_freq[None, :]
    ang_col = col_idx[:, None] * inv_freq[None, :]
    cr, sr, cc, sc = jnp.cos(ang_row), jnp.sin(ang_row), jnp.cos(ang_col), jnp.sin(ang_col)
    cos64 = jnp.concatenate([cr, cr, cc, cc], axis=-1)
    sin64 = jnp.concatenate([-sr, sr, -sc, sc], axis=-1)
    return jnp.tile(cos64, (1, 2)), jnp.tile(sin64, (1, 2))


def kernel(x, w_in, q_norm, k_norm, conv_w, conv_b, dt_bias_f, dt_bias_b, a_log_f, a_log_b, d_skip, ssd_norm, w_attn_proj, w_ssd_proj, w_out, norm1_pre, norm1_post, norm2_pre, norm2_post, w_gate_up, w_down):
    b, s, _ = x.shape
    assert w_in.shape[0] == 1, "single-layer block"
    assert s % TOK_BLOCK == 0 and s % Q_BLOCK == 0 and s % KV_BLOCK == 0 and s % CHUNK == 0
    x2 = x.reshape(b * s, D_MODEL)

    w = w_in[0]
    o_z = QK_WIDTH + KV_WIDTH
    o_xbc = o_z + SSD_D_INNER
    o_dt = o_xbc + CONV_CH
    o_g = o_dt + 2 * SSD_HEADS
    wqkv = w[:, :o_z].astype(BF16)
    wz = w[:, o_z:o_xbc].astype(BF16)
    wxbc = w[:, o_xbc:o_dt].astype(BF16)
    wdt = jnp.pad(w[:, o_dt:o_g], ((0, 0), (0, DT_PAD - 2 * SSD_HEADS))).astype(BF16)
    wg = w[:, o_g:].astype(BF16)
    row = lambda v: v.reshape(1, -1).astype(F32)
    qkg = jnp.concatenate([jnp.tile(q_norm[0], N_Q_HEADS), jnp.tile(k_norm[0], N_KV_HEADS)]).reshape(1, -1)
    cos_t, sin_t = _rope_tables(s)
    seg = np.arange(128) // HEAD_DIM
    bd = jnp.asarray(seg[:, None] == seg[None, :], dtype=BF16)

    q, k, v, z, xbc, gates, dt = _in_proj(x2, row(norm1_pre[0]), wqkv, wz, wxbc, wg, wdt, qkg,
                                          cos_t, sin_t, bd, s)

    qt = jnp.swapaxes(q.reshape(b, s, ATTN_WIDTH), 1, 2)
    vt = jnp.swapaxes(v.reshape(b, s, N_KV_HEADS, HEAD_DIM), 1, 3)
    vt = jnp.concatenate([jnp.swapaxes(vt, 1, 2), jnp.ones((b, N_KV_HEADS, ONES_ROWS, s), BF16)],
                         axis=2).reshape(b, N_KV_HEADS * V_ROWS, s)
    ot = _attention(qt, k.reshape(b, s, KV_WIDTH), vt)

    dt3 = dt.reshape(b, s, DT_PAD)
    dtt = jnp.swapaxes(dt3[:, :, :2 * SSD_HEADS], 1, 2)
    dt_bias = jnp.concatenate([dt_bias_f[0], dt_bias_b[0]]).astype(F32)
    a_log = jnp.concatenate([a_log_f[0], a_log_b[0]]).astype(F32)
    convw = jnp.pad(conv_w[0].astype(F32), ((0, 8 - CONV_K), (0, 0)))
    dskip = jnp.repeat(d_skip[0].astype(F32), SSD_HEADDIM).reshape(1, -1)
    hid = np.arange(SSD_D_INNER) // SSD_HEADDIM
    expand = jnp.asarray(np.arange(SSD_HEADS)[:, None] == hid[None, :], dtype=BF16)
    yf, yb = _ssd(xbc.reshape(b, s, CONV_CH), dt3, dtt, convw, row(conv_b[0]),
                  dt_bias.reshape(1, -1), dt_bias.reshape(-1, 1), a_log.reshape(1, -1),
                  a_log.reshape(-1, 1), dskip, expand)

    x1 = _merge(x2, yf.reshape(b * s, -1), yb.reshape(b * s, -1), z, ot, gates,
                w_ssd_proj[0].astype(BF16), w_attn_proj[0].astype(BF16), w_out[0].astype(BF16),
                row(ssd_norm[0]), row(norm1_post[0]), s)

    wgu = w_gate_up[0]
    out = _ffn(x1, row(norm2_pre[0]), wgu[:, :D_FF].astype(BF16), wgu[:, D_FF:].astype(BF16),
               w_down[0].astype(BF16), row(norm2_post[0]))
    return out.reshape(b, s, D_MODEL)
```

```python
import functools
import math

import jax
import jax.numpy as jnp
import numpy as np
from jax import lax
from jax.experimental import pallas as pl
from jax.experimental.pallas import tpu as pltpu

F32 = jnp.float32
BF16 = jnp.bfloat16

D_MODEL = 1024
GRID_W = 64
HEAD_DIM = 64
N_Q_HEADS = 8
N_KV_HEADS = 2
Q_PER_KV = N_Q_HEADS // N_KV_HEADS
ATTN_WIDTH = N_Q_HEADS * HEAD_DIM
KV_WIDTH = N_KV_HEADS * HEAD_DIM
QK_WIDTH = ATTN_WIDTH + KV_WIDTH
ROPE_THETA = 10000.0
AXIAL_DIM = HEAD_DIM // 2
SSD_D_INNER = D_MODEL
SSD_HEADDIM = 64
SSD_HEADS = SSD_D_INNER // SSD_HEADDIM
SSD_GROUPS = 2
HEADS_PER_GROUP = SSD_HEADS // SSD_GROUPS
GROUP_WIDTH = HEADS_PER_GROUP * SSD_HEADDIM
SSD_STATE = 128
CONV_K = 5
CONV_PAD = (CONV_K - 1) // 2
CHUNK = 128
CONV_CH = SSD_D_INNER + 2 * SSD_GROUPS * SSD_STATE
D_FF = ((8 * D_MODEL // 3 + 255) // 256) * 256
EPS = 1e-6
DT_PAD = 128
X_HALO = 8

VMEM_LIMIT = 56 * 1024 * 1024

TOK_BLOCK = 512
Q_BLOCK = 512
KV_BLOCK = 256
Q_TILE = 512
KV_UNROLL = 8
ONES_ROWS = 16
V_ROWS = HEAD_DIM + ONES_ROWS


def _rms(xf, gain):
    return xf * lax.rsqrt(jnp.mean(xf * xf, axis=-1, keepdims=True) + EPS) * gain


def _split_bf16(a, parts):
    out = []
    r = a
    for _ in range(parts):
        p = r.astype(BF16)
        out.append(p)
        r = r - p.astype(F32)
    return out


def _dot(a, b):
    return jnp.dot(a, b, preferred_element_type=F32)


def _in_proj_kernel(x_ref, xprev_ref, xnext_ref, g_ref, wqkv_ref, wz_ref, wxbc_ref, wg_ref, wdt_ref,
                    qkg_ref, cos_ref, sin_ref, bd_ref, convw_ref, convb_ref, qt_ref, k_ref, vt_ref,
                    z_ref, act_ref, gates_ref, dt_ref, dtt_ref, ext_scr, *, blocks_per_seq):
    tm = x_ref.shape[0]
    h = _rms(x_ref[...], g_ref[...]).astype(BF16)

    pos = pl.program_id(0) % blocks_per_seq
    x_halo = jnp.concatenate([xprev_ref[...], xnext_ref[...]], axis=0)
    halo = _dot(_rms(x_halo, g_ref[...]).astype(BF16), wxbc_ref[...])
    ext_scr[0:X_HALO, :] = halo[:X_HALO] * (pos > 0).astype(F32)
    ext_scr[X_HALO:X_HALO + tm, :] = _dot(h, wxbc_ref[...])
    ext_scr[X_HALO + tm:, :] = halo[X_HALO:] * (pos < blocks_per_seq - 1).astype(F32)

    z_ref[...] = _dot(h, wz_ref[...]).astype(BF16)
    gates_ref[...] = _dot(h, wg_ref[...]).astype(BF16)
    dt = _dot(h, wdt_ref[...])
    dt_ref[...] = dt
    dtt_ref[...] = dt.T[:2 * SSD_HEADS, :]
    qkv = _dot(h, wqkv_ref[...])
    vt = qkv[:, QK_WIDTH:].T
    for hd in range(N_KV_HEADS):
        vt_ref[hd, 0:HEAD_DIM, :] = vt[hd * HEAD_DIM:(hd + 1) * HEAD_DIM, :].astype(BF16)
        vt_ref[hd, HEAD_DIM:, :] = jnp.ones((ONES_ROWS, vt.shape[1]), BF16)

    qk = qkv[:, :QK_WIDTH]
    sq = (qk * qk).astype(BF16)
    bd = bd_ref[...]
    n_lane_tiles = QK_WIDTH // 128
    ms = jnp.concatenate([_dot(sq[:, i * 128:(i + 1) * 128], bd) for i in range(n_lane_tiles)],
                         axis=1) * (1.0 / HEAD_DIM)
    qn = qk * lax.rsqrt(ms + EPS) * qkg_ref[...]
    half = AXIAL_DIM // 2
    lane = lax.broadcasted_iota(jnp.int32, qn.shape, 1)
    partner = jnp.where((lane % AXIAL_DIM) < half,
                        pltpu.roll(qn, QK_WIDTH - half, 1), pltpu.roll(qn, half, 1))
    cos = jnp.concatenate([cos_ref[...]] * n_lane_tiles, axis=1)
    sin = jnp.concatenate([sin_ref[...]] * n_lane_tiles, axis=1)
    roped = qn * cos + partner * sin
    q = roped[:, :ATTN_WIDTH] * (math.log2(math.e) / math.sqrt(HEAD_DIM))
    qt_ref[...] = q.T.astype(BF16)
    k_ref[...] = roped[:, ATTN_WIDTH:].astype(BF16)

    conv = jnp.broadcast_to(convb_ref[...], (tm, CONV_CH))
    for k in range(CONV_K):
        conv = conv + convw_ref[k:k + 1, :] * ext_scr[pl.ds(X_HALO - CONV_PAD + k, tm), :]
    act_ref[...] = (conv * jax.nn.sigmoid(conv)).astype(BF16)


def _in_proj(x2, g, wqkv, wz, wxbc, wg, wdt, qkg, cos_t, sin_t, bd, convw, convb, batch, seq):
    n_tok = x2.shape[0]
    tm = TOK_BLOCK
    blocks_per_seq = seq // tm
    hpb = tm // X_HALO
    last_halo = n_tok // X_HALO - 1
    xprev = pl.BlockSpec((X_HALO, D_MODEL), lambda i: (jnp.maximum(i * hpb - 1, 0), 0))
    xnext = pl.BlockSpec((X_HALO, D_MODEL), lambda i: (jnp.minimum((i + 1) * hpb, last_halo), 0))
    full = lambda a: pl.BlockSpec(a.shape, lambda i: (0,) * a.ndim)
    row = lambda w: pl.BlockSpec((tm, w), lambda i: (i, 0))
    tab = pl.BlockSpec((tm, 128), lambda i: (i % blocks_per_seq, 0))
    col = lambda r: pl.BlockSpec((None, r, tm),
                                 lambda i: (i // blocks_per_seq, 0, i % blocks_per_seq))
    vt_spec = pl.BlockSpec((None, N_KV_HEADS, V_ROWS, tm),
                           lambda i: (i // blocks_per_seq, 0, 0, i % blocks_per_seq))
    out_shapes = (
        jax.ShapeDtypeStruct((batch, ATTN_WIDTH, seq), BF16),
        jax.ShapeDtypeStruct((n_tok, KV_WIDTH), BF16),
        jax.ShapeDtypeStruct((batch, N_KV_HEADS, V_ROWS, seq), BF16),
        jax.ShapeDtypeStruct((n_tok, SSD_D_INNER), BF16),
        jax.ShapeDtypeStruct((n_tok, CONV_CH), BF16),
        jax.ShapeDtypeStruct((n_tok, 2 * D_MODEL), BF16),
        jax.ShapeDtypeStruct((n_tok, DT_PAD), F32),
        jax.ShapeDtypeStruct((batch, 2 * SSD_HEADS, seq), F32),
    )
    return pl.pallas_call(
        functools.partial(_in_proj_kernel, blocks_per_seq=blocks_per_seq),
        grid=(n_tok // tm,),
        in_specs=[row(D_MODEL), xprev, xnext, full(g), full(wqkv), full(wz), full(wxbc), full(wg),
                  full(wdt), full(qkg), tab, tab, full(bd), full(convw), full(convb)],
        out_specs=(col(ATTN_WIDTH), row(KV_WIDTH), vt_spec, row(SSD_D_INNER), row(CONV_CH),
                   row(2 * D_MODEL), row(DT_PAD), col(2 * SSD_HEADS)),
        out_shape=out_shapes,
        scratch_shapes=[pltpu.VMEM((tm + 2 * X_HALO, CONV_CH), F32)],
        compiler_params=pltpu.CompilerParams(dimension_semantics=("arbitrary",),
                                             vmem_limit_bytes=VMEM_LIMIT),
        name="in_proj",
    )(x2, x2, x2, g, wqkv, wz, wxbc, wg, wdt, qkg, cos_t, sin_t, bd, convw, convb)


def _sublane_allreduce(x, op):
    for shift in (4, 2, 1):
        x = op(x, pltpu.roll(x, shift, 0))
    return x


def _attn_kernel(qt_ref, k_ref, vt_ref, o_ref, qpad_scr, m_scr, acc_scr, s_scr, cmax_scr, *,
                 n_kv_blocks):
    kvh = pl.program_id(1)
    bq = qt_ref.shape[1]
    n_tiles = (Q_PER_KV * bq) // Q_TILE
    q_tile = jnp.concatenate(
        [qt_ref[g * HEAD_DIM:(g + 1) * HEAD_DIM, :] for g in range(Q_PER_KV)], axis=1)
    zero = jnp.zeros_like(q_tile)
    qpad_scr[0:HEAD_DIM, :] = jnp.where(kvh == 0, q_tile, zero)
    qpad_scr[HEAD_DIM:, :] = jnp.where(kvh == 0, zero, q_tile)
    m_scr[...] = jnp.full(m_scr.shape, -1e30, F32)
    acc_scr[...] = jnp.zeros(acc_scr.shape, F32)

    def scores(start, t):
        return _dot(k_ref[pl.ds(start, KV_BLOCK), :], qpad_scr[:, t * Q_TILE:(t + 1) * Q_TILE])

    def colmax(s):
        return _sublane_allreduce(jnp.max(s.reshape(KV_BLOCK // 8, 8, Q_TILE), axis=0), jnp.maximum)

    def softmax_pv(t, s, cmax, vt_blk):
        cols = slice(t * Q_TILE, (t + 1) * Q_TILE)
        m_prev = m_scr[:, cols]
        m_new = jnp.maximum(m_prev, cmax)
        alpha = jnp.exp2(m_prev - m_new)
        p = jnp.exp2(s.reshape(KV_BLOCK // 8, 8, Q_TILE) - m_new[None])
        pv = _dot(vt_blk, p.reshape(KV_BLOCK, Q_TILE).astype(BF16))
        acc = acc_scr[:, cols].reshape(V_ROWS // 8, 8, Q_TILE) * alpha[None]
        acc_scr[:, cols] = acc.reshape(V_ROWS, Q_TILE) + pv
        m_scr[:, cols] = m_new

    assert n_tiles >= 2
    s_scr[0] = scores(0, 0)
    s_scr[1] = scores(0, 1)
    cmax_scr[...] = colmax(s_scr[0])

    def body(j, carry):
        start = pl.multiple_of(j * KV_BLOCK, KV_BLOCK)
        nxt = pl.multiple_of(jnp.minimum(j + 1, n_kv_blocks - 1) * KV_BLOCK, KV_BLOCK)
        vt_blk = vt_ref[:, pl.ds(start, KV_BLOCK)]
        s_cur, s_nxt, cmax_cur = s_scr[0], s_scr[1], cmax_scr[...]
        for t in range(n_tiles):
            ahead = t + 2
            s_new = scores(start, ahead) if ahead < n_tiles else scores(nxt, ahead - n_tiles)
            cmax_nxt = colmax(s_nxt)
            softmax_pv(t, s_cur, cmax_cur, vt_blk)
            s_cur, s_nxt, cmax_cur = s_nxt, s_new, cmax_nxt
        s_scr[0] = s_cur
        s_scr[1] = s_nxt
        cmax_scr[...] = cmax_cur
        return carry

    lax.fori_loop(0, n_kv_blocks, body, 0, unroll=KV_UNROLL)
    inv_l = 1.0 / acc_scr[HEAD_DIM:HEAD_DIM + 8, :]
    out = acc_scr[0:HEAD_DIM, :].reshape(HEAD_DIM // 8, 8, Q_PER_KV * bq) * inv_l[None]
    out = out.reshape(HEAD_DIM, Q_PER_KV * bq)
    for g in range(Q_PER_KV):
        o_ref[g * HEAD_DIM:(g + 1) * HEAD_DIM, :] = out[:, g * bq:(g + 1) * bq].astype(BF16)


def _attention(qt, k, vt):
    b, _, s = qt.shape
    bq = Q_BLOCK
    rows = Q_PER_KV * HEAD_DIM
    return pl.pallas_call(
        functools.partial(_attn_kernel, n_kv_blocks=s // KV_BLOCK),
        grid=(b, N_KV_HEADS, s // bq),
        in_specs=[pl.BlockSpec((None, rows, bq), lambda bi, h, qi: (bi, h, qi)),
                  pl.BlockSpec((None, s, KV_WIDTH), lambda bi, h, qi: (bi, 0, 0)),
                  pl.BlockSpec((None, None, V_ROWS, s), lambda bi, h, qi: (bi, h, 0, 0))],
        out_specs=pl.BlockSpec((None, rows, bq), lambda bi, h, qi: (bi, h, qi)),
        out_shape=jax.ShapeDtypeStruct((b, ATTN_WIDTH, s), BF16),
        scratch_shapes=[pltpu.VMEM((KV_WIDTH, Q_PER_KV * bq), BF16),
                        pltpu.VMEM((8, Q_PER_KV * bq), F32),
                        pltpu.VMEM((V_ROWS, Q_PER_KV * bq), F32),
                        pltpu.VMEM((2, KV_BLOCK, Q_TILE), F32),
                        pltpu.VMEM((8, Q_TILE), F32)],
        compiler_params=pltpu.CompilerParams(
            dimension_semantics=("arbitrary", "arbitrary", "arbitrary"),
            vmem_limit_bytes=VMEM_LIMIT),
        name="attention",
    )(qt, k, vt)


def _softplus(x):
    return jnp.maximum(x, 0.0) + jnp.log1p(jnp.exp(-jnp.abs(x)))


def _ssd_direction(d, act_ref, dt_ref, dtt_ref, dtb_row_ref, dtb_col_ref, alog_row_ref,
                   alog_col_ref, dskip_ref, expand_ref, h_scr, y_ref):
    hs = slice(d * SSD_HEADS, (d + 1) * SSD_HEADS)
    xs = act_ref[:, :SSD_D_INNER].astype(F32)

    dt_c = _softplus(dt_ref[:, hs] + dtb_row_ref[:, hs])
    dt_r = _softplus(dtt_ref[hs, :] + dtb_col_ref[hs, :])
    adt_c = dt_c * (-jnp.exp(alog_row_ref[:, hs]))
    adt_r = dt_r * (-jnp.exp(alog_col_ref[hs, :]))
    ri = lax.broadcasted_iota(jnp.int32, (CHUNK, CHUNK), 0)
    ci = lax.broadcasted_iota(jnp.int32, (CHUNK, CHUNK), 1)
    lower = ri >= ci
    upper = ri <= ci
    before = upper if d else lower
    tri = jnp.where(before, 1.0, 0.0).astype(BF16)
    tri_t = jnp.where(lower if d else upper, 1.0, 0.0).astype(BF16)
    c_col = sum(_dot(tri, p) for p in _split_bf16(adt_c, 3))
    c_row = sum(_dot(p, tri_t) for p in _split_bf16(adt_r, 3))
    end = 0 if d else CHUNK - 1
    c_end = c_col[end:end + 1, :]
    expand = expand_ref[...]
    dt_x = _dot(dt_c.astype(BF16), expand)
    ec_x = _dot(jnp.exp(c_col).astype(BF16), expand)
    ed_x = _dot(jnp.exp(c_end - c_col).astype(BF16), expand)
    eend = jnp.broadcast_to(jnp.exp(c_end), (8, SSD_HEADS))
    eend_x = sum(_dot(p, expand) for p in _split_bf16(eend, 3))[0:1, :]

    xd = xs * dt_x
    xd_b = xd.astype(BF16)
    xde_b = (xd * ed_x).astype(BF16)
    y_groups = []
    pair_w = 2 * SSD_HEADDIM
    lane = lax.broadcasted_iota(jnp.int32, (CHUNK, pair_w), 1)
    zero_b = jnp.zeros((CHUNK, pair_w), BF16)
    for g in range(SSD_GROUPS):
        gs = slice(g * GROUP_WIDTH, (g + 1) * GROUP_WIDTH)
        b_g = act_ref[:, SSD_D_INNER + g * SSD_STATE:SSD_D_INNER + (g + 1) * SSD_STATE]
        c_off = SSD_D_INNER + SSD_GROUPS * SSD_STATE
        c_g = act_ref[:, c_off + g * SSD_STATE:c_off + (g + 1) * SSD_STATE]
        cb = lax.dot_general(c_g, b_g, (((1,), (1,)), ((), ())), preferred_element_type=F32)
        y_pairs = []
        for r in range(0, HEADS_PER_GROUP, 2):
            ws = []
            for hd in (g * HEADS_PER_GROUP + r, g * HEADS_PER_GROUP + r + 1):
                seg = c_col[:, hd:hd + 1] - c_row[hd:hd + 1, :]
                ws.append((cb * jnp.where(before, jnp.exp(seg), 0.0)).astype(BF16))
            pc = (g * HEADS_PER_GROUP + r) * SSD_HEADDIM
            xd_pair = xd_b[:, pc:pc + pair_w]
            diag = jnp.concatenate([jnp.where(lane < SSD_HEADDIM, xd_pair, zero_b),
                                    jnp.where(lane < SSD_HEADDIM, zero_b, xd_pair)], axis=0)
            y_pairs.append(_dot(jnp.concatenate(ws, axis=1), diag))
        h_in = h_scr[g]
        y_off = _dot(c_g, h_in.astype(BF16)) * ec_x[:, gs]
        y_groups.append(jnp.concatenate(y_pairs, axis=1) + y_off)
        upd = lax.dot_general(b_g, xde_b[:, gs], (((0,), (0,)), ((), ())),
                              preferred_element_type=F32)
        h_scr[g] = h_in * eend_x[:, gs] + upd
    y = jnp.concatenate(y_groups, axis=1)
    if d == 0:
        y = y + dskip_ref[...] * xs
    y_ref[...] = y.astype(BF16)


def _ssd_kernel(actf, actb, dtf, dttf, dtb, dttb, dtb_row, dtb_col, alog_row, alog_col, dskip,
                expand, yf_ref, yb_ref, hf_scr, hb_scr):
    @pl.when(pl.program_id(1) == 0)
    def _():
        hf_scr[...] = jnp.zeros(hf_scr.shape, F32)
        hb_scr[...] = jnp.zeros(hb_scr.shape, F32)

    common = (dtb_row, dtb_col, alog_row, alog_col, dskip, expand)
    _ssd_direction(0, actf, dtf, dttf, *common, hf_scr, yf_ref)
    _ssd_direction(1, actb, dtb, dttb, *common, hb_scr, yb_ref)


def _ssd(act, dt, dtt, dtb_row, dtb_col, alog_row, alog_col, dskip, expand):
    b, s, _ = act.shape
    nc = s // CHUNK
    fwd = lambda i: i
    bwd = lambda i: nc - 1 - i
    act_spec = lambda c: pl.BlockSpec((None, CHUNK, CONV_CH), lambda bi, i: (bi, c(i), 0))
    dt_spec = lambda c: pl.BlockSpec((None, CHUNK, DT_PAD), lambda bi, i: (bi, c(i), 0))
    dtt_spec = lambda c: pl.BlockSpec((None, 2 * SSD_HEADS, CHUNK), lambda bi, i: (bi, 0, c(i)))
    y_spec = lambda c: pl.BlockSpec((None, CHUNK, SSD_D_INNER), lambda bi, i: (bi, c(i), 0))
    full = lambda a: pl.BlockSpec(a.shape, lambda bi, i: (0,) * a.ndim)
    consts = (dtb_row, dtb_col, alog_row, alog_col, dskip, expand)
    y_shape = jax.ShapeDtypeStruct((b, s, SSD_D_INNER), BF16)
    return pl.pallas_call(
        _ssd_kernel,
        grid=(b, nc),
        in_specs=[act_spec(fwd), act_spec(bwd), dt_spec(fwd), dtt_spec(fwd), dt_spec(bwd),
                  dtt_spec(bwd)] + [full(a) for a in consts],
        out_specs=(y_spec(fwd), y_spec(bwd)),
        out_shape=(y_shape, y_shape),
        scratch_shapes=[pltpu.VMEM((SSD_GROUPS, SSD_STATE, GROUP_WIDTH), F32),
                        pltpu.VMEM((SSD_GROUPS, SSD_STATE, GROUP_WIDTH), F32)],
        compiler_params=pltpu.CompilerParams(dimension_semantics=("arbitrary", "arbitrary"),
                                             vmem_limit_bytes=VMEM_LIMIT),
        name="ssd",
    )(act, act, dt, dtt, dt, dtt, *consts)


def _merge_kernel(x_ref, yf_ref, yb_ref, z_ref, ot_ref, gates_ref, wssd_ref, wattn_ref, wout_ref,
                  gssd_ref, gpost_ref, o_ref):
    y = yf_ref[...].astype(F32) + yb_ref[...].astype(F32)
    z = z_ref[...].astype(F32)
    ssd_in = _rms(y * (z * jax.nn.sigmoid(z)), gssd_ref[...]).astype(BF16)
    ssd_out = _dot(ssd_in, wssd_ref[...])
    attn_out = lax.dot_general(ot_ref[...], wattn_ref[...], (((0,), (0,)), ((), ())),
                               preferred_element_type=F32)
    gates = jax.nn.sigmoid(gates_ref[...].astype(F32))
    mixed = gates[:, :D_MODEL] * attn_out + gates[:, D_MODEL:] * ssd_out
    proj = _dot(mixed.astype(BF16), wout_ref[...])
    o_ref[...] = x_ref[...] + _rms(proj, gpost_ref[...])


def _merge(x2, yf, yb, z, ot, gates, wssd, wattn, wout, gssd, gpost, seq):
    n_tok = x2.shape[0]
    tm = TOK_BLOCK
    bps = seq // tm
    full = lambda a: pl.BlockSpec(a.shape, lambda i: (0,) * a.ndim)
    row = lambda w: pl.BlockSpec((tm, w), lambda i: (i, 0))
    return pl.pallas_call(
        _merge_kernel,
        grid=(n_tok // tm,),
        in_specs=[row(D_MODEL), row(SSD_D_INNER), row(SSD_D_INNER), row(SSD_D_INNER),
                  pl.BlockSpec((None, ATTN_WIDTH, tm), lambda i: (i // bps, 0, i % bps)),
                  row(2 * D_MODEL), full(wssd), full(wattn), full(wout), full(gssd), full(gpost)],
        out_specs=row(D_MODEL),
        out_shape=jax.ShapeDtypeStruct((n_tok, D_MODEL), F32),
        compiler_params=pltpu.CompilerParams(dimension_semantics=("arbitrary",),
                                             vmem_limit_bytes=VMEM_LIMIT),
        name="merge",
    )(x2, yf, yb, z, ot, gates, wssd, wattn, wout, gssd, gpost)


def _ffn_kernel(x_ref, gpre_ref, wgate_ref, wup_ref, wdown_ref, gpost_ref, o_ref):
    x = x_ref[...]
    h = _rms(x, gpre_ref[...]).astype(BF16)
    gate = _dot(h, wgate_ref[...])
    up = _dot(h, wup_ref[...])
    act = (gate * jax.nn.sigmoid(gate) * up).astype(BF16)
    ffn = _dot(act, wdown_ref[...])
    o_ref[...] = x + _rms(ffn, gpost_ref[...])


def _ffn(x2, gpre, wgate, wup, wdown, gpost):
    n_tok = x2.shape[0]
    tm = TOK_BLOCK
    full = lambda a: pl.BlockSpec(a.shape, lambda i: (0,) * a.ndim)
    row = pl.BlockSpec((tm, D_MODEL), lambda i: (i, 0))
    return pl.pallas_call(
        _ffn_kernel,
        grid=(n_tok // tm,),
        in_specs=[row, full(gpre), full(wgate), full(wup), full(wdown), full(gpost)],
        out_specs=row,
        out_shape=jax.ShapeDtypeStruct((n_tok, D_MODEL), F32),
        compiler_params=pltpu.CompilerParams(dimension_semantics=("arbitrary",),
                                             vmem_limit_bytes=VMEM_LIMIT),
        name="ffn",
    )(x2, gpre, wgate, wup, wdown, gpost)


def _rope_tables(seq):
    rows = seq // GRID_W
    row_idx = jnp.repeat(jnp.arange(rows, dtype=F32), GRID_W)
    col_idx = jnp.tile(jnp.arange(GRID_W, dtype=F32), rows)
    inv_freq = ROPE_THETA ** (-jnp.arange(0, AXIAL_DIM, 2, dtype=F32) / AXIAL_DIM)
    ang_row = row_idx[:, None] * inv_freq[None, :]
    ang_col = col_idx[:, None] * inv_freq[None, :]
    cr, sr, cc, sc = jnp.cos(ang_row), jnp.sin(ang_row), jnp.cos(ang_col), jnp.sin(ang_col)
    cos64 = jnp.concatenate([cr, cr, cc, cc], axis=-1)
    sin64 = jnp.concatenate([-sr, sr, -sc, sc], axis=-1)
    return jnp.tile(cos64, (1, 2)), jnp.tile(sin64, (1, 2))


def kernel(x, w_in, q_norm, k_norm, conv_w, conv_b, dt_bias_f, dt_bias_b, a_log_f, a_log_b, d_skip, ssd_norm, w_attn_proj, w_ssd_proj, w_out, norm1_pre, norm1_post, norm2_pre, norm2_post, w_gate_up, w_down):
    b, s, _ = x.shape
    assert w_in.shape[0] == 1, "single-layer block"
    assert s % TOK_BLOCK == 0 and s % Q_BLOCK == 0 and s % KV_BLOCK == 0 and s % CHUNK == 0
    x2 = x.reshape(b * s, D_MODEL)

    w = w_in[0]
    o_z = QK_WIDTH + KV_WIDTH
    o_xbc = o_z + SSD_D_INNER
    o_dt = o_xbc + CONV_CH
    o_g = o_dt + 2 * SSD_HEADS
    wqkv = w[:, :o_z].astype(BF16)
    wz = w[:, o_z:o_xbc].astype(BF16)
    wxbc = w[:, o_xbc:o_dt].astype(BF16)
    wdt = jnp.pad(w[:, o_dt:o_g], ((0, 0), (0, DT_PAD - 2 * SSD_HEADS))).astype(BF16)
    wg = w[:, o_g:].astype(BF16)
    row = lambda v: v.reshape(1, -1).astype(F32)
    qkg = jnp.concatenate([jnp.tile(q_norm[0], N_Q_HEADS), jnp.tile(k_norm[0], N_KV_HEADS)]).reshape(1, -1)
    cos_t, sin_t = _rope_tables(s)
    seg = np.arange(128) // HEAD_DIM
    bd = jnp.asarray(seg[:, None] == seg[None, :], dtype=BF16)

    convw = jnp.pad(conv_w[0].astype(F32), ((0, 8 - CONV_K), (0, 0)))
    qt, k, vt, z, act, gates, dt, dtt = _in_proj(x2, row(norm1_pre[0]), wqkv, wz, wxbc, wg, wdt,
                                                 qkg, cos_t, sin_t, bd, convw, row(conv_b[0]), b, s)
    ot = _attention(qt, k.reshape(b, s, KV_WIDTH), vt)

    dt3 = dt.reshape(b, s, DT_PAD)
    dt_bias = jnp.concatenate([dt_bias_f[0], dt_bias_b[0]]).astype(F32)
    a_log = jnp.concatenate([a_log_f[0], a_log_b[0]]).astype(F32)
    dskip = jnp.repeat(d_skip[0].astype(F32), SSD_HEADDIM).reshape(1, -1)
    hid = np.arange(SSD_D_INNER) // SSD_HEADDIM
    expand = jnp.asarray(np.arange(SSD_HEADS)[:, None] == hid[None, :], dtype=BF16)
    yf, yb = _ssd(act.reshape(b, s, CONV_CH), dt3, dtt, dt_bias.reshape(1, -1), dt_bias.reshape(-1, 1),
                  a_log.reshape(1, -1), a_log.reshape(-1, 1), dskip, expand)

    x1 = _merge(x2, yf.reshape(b * s, -1), yb.reshape(b * s, -1), z, ot, gates,
                w_ssd_proj[0].astype(BF16), w_attn_proj[0].astype(BF16), w_out[0].astype(BF16),
                row(ssd_norm[0]), row(norm1_post[0]), s)

    wgu = w_gate_up[0]
    out = _ffn(x1, row(norm2_pre[0]), wgu[:, :D_FF].astype(BF16), wgu[:, D_FF:].astype(BF16),
               w_down[0].astype(BF16), row(norm2_post[0]))
    return out.reshape(b, s, D_MODEL)
```

```python
import functools
import math

import jax
import jax.numpy as jnp
import numpy as np
from jax import lax
from jax.experimental import pallas as pl
from jax.experimental.pallas import tpu as pltpu

F32 = jnp.float32
BF16 = jnp.bfloat16

D_MODEL = 1024
GRID_W = 64
HEAD_DIM = 64
N_Q_HEADS = 8
N_KV_HEADS = 2
Q_PER_KV = N_Q_HEADS // N_KV_HEADS
ATTN_WIDTH = N_Q_HEADS * HEAD_DIM
KV_WIDTH = N_KV_HEADS * HEAD_DIM
QK_WIDTH = ATTN_WIDTH + KV_WIDTH
ROPE_THETA = 10000.0
AXIAL_DIM = HEAD_DIM // 2
SSD_D_INNER = D_MODEL
SSD_HEADDIM = 64
SSD_HEADS = SSD_D_INNER // SSD_HEADDIM
SSD_GROUPS = 2
HEADS_PER_GROUP = SSD_HEADS // SSD_GROUPS
GROUP_WIDTH = HEADS_PER_GROUP * SSD_HEADDIM
SSD_STATE = 128
CONV_K = 5
CONV_PAD = (CONV_K - 1) // 2
CHUNK = 128
CONV_CH = SSD_D_INNER + 2 * SSD_GROUPS * SSD_STATE
D_FF = ((8 * D_MODEL // 3 + 255) // 256) * 256
EPS = 1e-6
DT_PAD = 128
X_HALO = 8

VMEM_LIMIT = 56 * 1024 * 1024

TOK_BLOCK = 512
MERGE_SPLIT = 4
Q_BLOCK = 512
KV_BLOCK = 256
Q_TILE = 512
KV_UNROLL = 16
ONES_ROWS = 16
V_ROWS = HEAD_DIM + ONES_ROWS


def _rms(xf, gain):
    return xf * lax.rsqrt(jnp.mean(xf * xf, axis=-1, keepdims=True) + EPS) * gain


def _split_bf16(a, parts):
    out = []
    r = a
    for _ in range(parts):
        p = r.astype(BF16)
        out.append(p)
        r = r - p.astype(F32)
    return out


def _dot(a, b):
    return jnp.dot(a, b, preferred_element_type=F32)


def _in_proj_kernel(x_ref, xprev_ref, xnext_ref, g_ref, wqkv_ref, wz_ref, wxbc_ref, wg_ref, wdt_ref,
                    qkg_ref, cos_ref, sin_ref, bd_ref, convw_ref, convb_ref, qt_ref, k_ref, vt_ref,
                    z_ref, act_ref, gates_ref, dt_ref, dtt_ref, ext_scr, *, blocks_per_seq):
    tm = x_ref.shape[0]
    h = _rms(x_ref[...], g_ref[...]).astype(BF16)

    pos = pl.program_id(0) % blocks_per_seq
    x_halo = jnp.concatenate([xprev_ref[...], xnext_ref[...]], axis=0)
    halo = _dot(_rms(x_halo, g_ref[...]).astype(BF16), wxbc_ref[...])
    ext_scr[0:X_HALO, :] = halo[:X_HALO] * (pos > 0).astype(F32)
    ext_scr[X_HALO:X_HALO + tm, :] = _dot(h, wxbc_ref[...])
    ext_scr[X_HALO + tm:, :] = halo[X_HALO:] * (pos < blocks_per_seq - 1).astype(F32)

    z_ref[...] = _dot(h, wz_ref[...]).astype(BF16)
    gates_ref[...] = _dot(h, wg_ref[...]).astype(BF16)
    dt = _dot(h, wdt_ref[...])
    dt_ref[...] = dt
    dtt_ref[...] = dt.T[:2 * SSD_HEADS, :]
    qkv = _dot(h, wqkv_ref[...])
    vt = qkv[:, QK_WIDTH:].T
    for hd in range(N_KV_HEADS):
        vt_ref[hd, 0:HEAD_DIM, :] = vt[hd * HEAD_DIM:(hd + 1) * HEAD_DIM, :].astype(BF16)
        vt_ref[hd, HEAD_DIM:, :] = jnp.ones((ONES_ROWS, vt.shape[1]), BF16)

    qk = qkv[:, :QK_WIDTH]
    sq = (qk * qk).astype(BF16)
    bd = bd_ref[...]
    n_lane_tiles = QK_WIDTH // 128
    ms = jnp.concatenate([_dot(sq[:, i * 128:(i + 1) * 128], bd) for i in range(n_lane_tiles)],
                         axis=1) * (1.0 / HEAD_DIM)
    qn = qk * lax.rsqrt(ms + EPS) * qkg_ref[...]
    half = AXIAL_DIM // 2
    lane = lax.broadcasted_iota(jnp.int32, qn.shape, 1)
    partner = jnp.where((lane % AXIAL_DIM) < half,
                        pltpu.roll(qn, QK_WIDTH - half, 1), pltpu.roll(qn, half, 1))
    cos = jnp.concatenate([cos_ref[...]] * n_lane_tiles, axis=1)
    sin = jnp.concatenate([sin_ref[...]] * n_lane_tiles, axis=1)
    roped = qn * cos + partner * sin
    q = roped[:, :ATTN_WIDTH] * (math.log2(math.e) / math.sqrt(HEAD_DIM))
    qt_ref[...] = q.T.astype(BF16)
    k_ref[...] = roped[:, ATTN_WIDTH:].astype(BF16)

    conv = jnp.broadcast_to(convb_ref[...], (tm, CONV_CH))
    for k in range(CONV_K):
        conv = conv + convw_ref[k:k + 1, :] * ext_scr[pl.ds(X_HALO - CONV_PAD + k, tm), :]
    act_ref[...] = (conv * jax.nn.sigmoid(conv)).astype(BF16)


def _in_proj(x2, g, wqkv, wz, wxbc, wg, wdt, qkg, cos_t, sin_t, bd, convw, convb, batch, seq):
    n_tok = x2.shape[0]
    tm = TOK_BLOCK
    blocks_per_seq = seq // tm
    hpb = tm // X_HALO
    last_halo = n_tok // X_HALO - 1
    xprev = pl.BlockSpec((X_HALO, D_MODEL), lambda i: (jnp.maximum(i * hpb - 1, 0), 0))
    xnext = pl.BlockSpec((X_HALO, D_MODEL), lambda i: (jnp.minimum((i + 1) * hpb, last_halo), 0))
    full = lambda a: pl.BlockSpec(a.shape, lambda i: (0,) * a.ndim)
    row = lambda w: pl.BlockSpec((tm, w), lambda i: (i, 0))
    tab = pl.BlockSpec((tm, 128), lambda i: (i % blocks_per_seq, 0))
    col = lambda r: pl.BlockSpec((None, r, tm),
                                 lambda i: (i // blocks_per_seq, 0, i % blocks_per_seq))
    vt_spec = pl.BlockSpec((None, N_KV_HEADS, V_ROWS, tm),
                           lambda i: (i // blocks_per_seq, 0, 0, i % blocks_per_seq))
    out_shapes = (
        jax.ShapeDtypeStruct((batch, ATTN_WIDTH, seq), BF16),
        jax.ShapeDtypeStruct((n_tok, KV_WIDTH), BF16),
        jax.ShapeDtypeStruct((batch, N_KV_HEADS, V_ROWS, seq), BF16),
        jax.ShapeDtypeStruct((n_tok, SSD_D_INNER), BF16),
        jax.ShapeDtypeStruct((n_tok, CONV_CH), BF16),
        jax.ShapeDtypeStruct((n_tok, 2 * D_MODEL), BF16),
        jax.ShapeDtypeStruct((n_tok, DT_PAD), F32),
        jax.ShapeDtypeStruct((batch, 2 * SSD_HEADS, seq), F32),
    )
    return pl.pallas_call(
        functools.partial(_in_proj_kernel, blocks_per_seq=blocks_per_seq),
        grid=(n_tok // tm,),
        in_specs=[row(D_MODEL), xprev, xnext, full(g), full(wqkv), full(wz), full(wxbc), full(wg),
                  full(wdt), full(qkg), tab, tab, full(bd), full(convw), full(convb)],
        out_specs=(col(ATTN_WIDTH), row(KV_WIDTH), vt_spec, row(SSD_D_INNER), row(CONV_CH),
                   row(2 * D_MODEL), row(DT_PAD), col(2 * SSD_HEADS)),
        out_shape=out_shapes,
        scratch_shapes=[pltpu.VMEM((tm + 2 * X_HALO, CONV_CH), F32)],
        compiler_params=pltpu.CompilerParams(dimension_semantics=("arbitrary",),
                                             vmem_limit_bytes=VMEM_LIMIT),
        name="in_proj",
    )(x2, x2, x2, g, wqkv, wz, wxbc, wg, wdt, qkg, cos_t, sin_t, bd, convw, convb)


def _sublane_allreduce(x, op):
    for shift in (4, 2, 1):
        x = op(x, pltpu.roll(x, shift, 0))
    return x


def _attn_kernel(qt_ref, k_ref, vt_ref, o_ref, qpad_scr, m_scr, acc_scr, s_scr, cmax_scr, *,
                 n_kv_blocks):
    kvh = pl.program_id(1)
    bq = qt_ref.shape[1]
    n_tiles = (Q_PER_KV * bq) // Q_TILE
    q_tile = jnp.concatenate(
        [qt_ref[g * HEAD_DIM:(g + 1) * HEAD_DIM, :] for g in range(Q_PER_KV)], axis=1)
    zero = jnp.zeros_like(q_tile)
    qpad_scr[0:HEAD_DIM, :] = jnp.where(kvh == 0, q_tile, zero)
    qpad_scr[HEAD_DIM:, :] = jnp.where(kvh == 0, zero, q_tile)
    m_scr[...] = jnp.full(m_scr.shape, -1e30, F32)
    acc_scr[...] = jnp.zeros(acc_scr.shape, F32)

    def scores(start, t):
        return _dot(k_ref[pl.ds(start, KV_BLOCK), :], qpad_scr[:, t * Q_TILE:(t + 1) * Q_TILE])

    def colmax(s):
        return _sublane_allreduce(jnp.max(s.reshape(KV_BLOCK // 8, 8, Q_TILE), axis=0), jnp.maximum)

    def softmax_pv(t, s, cmax, vt_blk):
        cols = slice(t * Q_TILE, (t + 1) * Q_TILE)
        m_prev = m_scr[:, cols]
        m_new = jnp.maximum(m_prev, cmax)
        alpha = jnp.exp2(m_prev - m_new)
        p = jnp.exp2(s.reshape(KV_BLOCK // 8, 8, Q_TILE) - m_new[None])
        pv = _dot(vt_blk, p.reshape(KV_BLOCK, Q_TILE).astype(BF16))
        acc = acc_scr[:, cols].reshape(V_ROWS // 8, 8, Q_TILE) * alpha[None]
        acc_scr[:, cols] = acc.reshape(V_ROWS, Q_TILE) + pv
        m_scr[:, cols] = m_new

    assert n_tiles >= 2
    s_scr[0] = scores(0, 0)
    s_scr[1] = scores(0, 1)
    cmax_scr[...] = colmax(s_scr[0])

    def body(j, carry):
        start = pl.multiple_of(j * KV_BLOCK, KV_BLOCK)
        nxt = pl.multiple_of(jnp.minimum(j + 1, n_kv_blocks - 1) * KV_BLOCK, KV_BLOCK)
        vt_blk = vt_ref[:, pl.ds(start, KV_BLOCK)]
        s_cur, s_nxt, cmax_cur = s_scr[0], s_scr[1], cmax_scr[...]
        for t in range(n_tiles):
            ahead = t + 2
            s_new = scores(start, ahead) if ahead < n_tiles else scores(nxt, ahead - n_tiles)
            cmax_nxt = colmax(s_nxt)
            softmax_pv(t, s_cur, cmax_cur, vt_blk)
            s_cur, s_nxt, cmax_cur = s_nxt, s_new, cmax_nxt
        s_scr[0] = s_cur
        s_scr[1] = s_nxt
        cmax_scr[...] = cmax_cur
        return carry

    lax.fori_loop(0, n_kv_blocks, body, 0, unroll=KV_UNROLL)
    inv_l = 1.0 / acc_scr[HEAD_DIM:HEAD_DIM + 8, :]
    out = acc_scr[0:HEAD_DIM, :].reshape(HEAD_DIM // 8, 8, Q_PER_KV * bq) * inv_l[None]
    out = out.reshape(HEAD_DIM, Q_PER_KV * bq)
    for g in range(Q_PER_KV):
        o_ref[g * HEAD_DIM:(g + 1) * HEAD_DIM, :] = out[:, g * bq:(g + 1) * bq].astype(BF16)


def _attention(qt, k, vt):
    b, _, s = qt.shape
    bq = Q_BLOCK
    rows = Q_PER_KV * HEAD_DIM
    return pl.pallas_call(
        functools.partial(_attn_kernel, n_kv_blocks=s // KV_BLOCK),
        grid=(b, N_KV_HEADS, s // bq),
        in_specs=[pl.BlockSpec((None, rows, bq), lambda bi, h, qi: (bi, h, qi)),
                  pl.BlockSpec((None, s, KV_WIDTH), lambda bi, h, qi: (bi, 0, 0)),
                  pl.BlockSpec((None, None, V_ROWS, s), lambda bi, h, qi: (bi, h, 0, 0))],
        out_specs=pl.BlockSpec((None, rows, bq), lambda bi, h, qi: (bi, h, qi)),
        out_shape=jax.ShapeDtypeStruct((b, ATTN_WIDTH, s), BF16),
        scratch_shapes=[pltpu.VMEM((KV_WIDTH, Q_PER_KV * bq), BF16),
                        pltpu.VMEM((8, Q_PER_KV * bq), F32),
                        pltpu.VMEM((V_ROWS, Q_PER_KV * bq), F32),
                        pltpu.VMEM((2, KV_BLOCK, Q_TILE), F32),
                        pltpu.VMEM((8, Q_TILE), F32)],
        compiler_params=pltpu.CompilerParams(
            dimension_semantics=("arbitrary", "arbitrary", "arbitrary"),
            vmem_limit_bytes=VMEM_LIMIT),
        name="attention",
    )(qt, k, vt)


def _softplus(x):
    return jnp.maximum(x, 0.0) + jnp.log1p(jnp.exp(-jnp.abs(x)))


def _ssd_direction(d, act_ref, dt_ref, dtt_ref, dtb_row_ref, dtb_col_ref, alog_row_ref,
                   alog_col_ref, dskip_ref, expand_ref, h_scr, y_ref):
    hs = slice(d * SSD_HEADS, (d + 1) * SSD_HEADS)
    c_off = SSD_D_INNER + SSD_GROUPS * SSD_STATE
    b_gs = [act_ref[:, SSD_D_INNER + g * SSD_STATE:SSD_D_INNER + (g + 1) * SSD_STATE]
            for g in range(SSD_GROUPS)]
    c_gs = [act_ref[:, c_off + g * SSD_STATE:c_off + (g + 1) * SSD_STATE]
            for g in range(SSD_GROUPS)]

    cbs = [lax.dot_general(c_gs[g], b_gs[g], (((1,), (1,)), ((), ())), preferred_element_type=F32)
           for g in range(SSD_GROUPS)]
    h_ins = [h_scr[g] for g in range(SSD_GROUPS)]
    y_offs = [_dot(c_gs[g], h_ins[g].astype(BF16)) for g in range(SSD_GROUPS)]
    yield

    dt_c = _softplus(dt_ref[:, hs] + dtb_row_ref[:, hs])
    dt_r = _softplus(dtt_ref[hs, :] + dtb_col_ref[hs, :])
    adt_c = dt_c * (-jnp.exp(alog_row_ref[:, hs]))
    adt_r = dt_r * (-jnp.exp(alog_col_ref[hs, :]))
    ri = lax.broadcasted_iota(jnp.int32, (CHUNK, CHUNK), 0)
    ci = lax.broadcasted_iota(jnp.int32, (CHUNK, CHUNK), 1)
    lower = ri >= ci
    upper = ri <= ci
    before = upper if d else lower
    tri = jnp.where(before, 1.0, 0.0).astype(BF16)
    tri_t = jnp.where(lower if d else upper, 1.0, 0.0).astype(BF16)
    c_col = sum(_dot(tri, p) for p in _split_bf16(adt_c, 3))
    c_row = sum(_dot(p, tri_t) for p in _split_bf16(adt_r, 3))
    end = 0 if d else CHUNK - 1
    c_end = c_col[end:end + 1, :]
    yield

    expand = expand_ref[...]
    dt_x = _dot(dt_c.astype(BF16), expand)
    ec_x = _dot(jnp.exp(c_col).astype(BF16), expand)
    ed_x = _dot(jnp.exp(c_end - c_col).astype(BF16), expand)
    eend = jnp.broadcast_to(jnp.exp(c_end), (8, SSD_HEADS))
    eend_x = sum(_dot(p, expand) for p in _split_bf16(eend, 3))[0:1, :]
    xs = act_ref[:, :SSD_D_INNER].astype(F32)
    xd = xs * dt_x
    xd_b = xd.astype(BF16)
    xde_b = (xd * ed_x).astype(BF16)
    yield

    pair_w = 2 * SSD_HEADDIM
    lane = lax.broadcasted_iota(jnp.int32, (CHUNK, pair_w), 1)
    zero_b = jnp.zeros((CHUNK, pair_w), BF16)
    y_pairs = []
    for hd0 in range(0, SSD_HEADS, 2):
        cb = cbs[hd0 // HEADS_PER_GROUP]
        ws = []
        for hd in (hd0, hd0 + 1):
            seg = c_col[:, hd:hd + 1] - c_row[hd:hd + 1, :]
            ws.append((cb * jnp.where(before, jnp.exp(seg), 0.0)).astype(BF16))
        xd_pair = xd_b[:, hd0 * SSD_HEADDIM:hd0 * SSD_HEADDIM + pair_w]
        diag = jnp.concatenate([jnp.where(lane < SSD_HEADDIM, xd_pair, zero_b),
                                jnp.where(lane < SSD_HEADDIM, zero_b, xd_pair)], axis=0)
        y_pairs.append(_dot(jnp.concatenate(ws, axis=1), diag))
        yield

    for g in range(SSD_GROUPS):
        gs = slice(g * GROUP_WIDTH, (g + 1) * GROUP_WIDTH)
        upd = lax.dot_general(b_gs[g], xde_b[:, gs], (((0,), (0,)), ((), ())),
                              preferred_element_type=F32)
        h_scr[g] = h_ins[g] * eend_x[:, gs] + upd
    y = jnp.concatenate(y_pairs, axis=1) + jnp.concatenate(y_offs, axis=1) * ec_x
    if d == 0:
        y = y + dskip_ref[...] * xs
    y_ref[...] = y.astype(BF16)


def _ssd_kernel(actf, actb, dtf, dttf, dtb, dttb, dtb_row, dtb_col, alog_row, alog_col, dskip,
                expand, yf_ref, yb_ref, hf_scr, hb_scr):
    @pl.when(pl.program_id(1) == 0)
    def _():
        hf_scr[...] = jnp.zeros(hf_scr.shape, F32)
        hb_scr[...] = jnp.zeros(hb_scr.shape, F32)

    common = (dtb_row, dtb_col, alog_row, alog_col, dskip, expand)
    stages = [_ssd_direction(0, actf, dtf, dttf, *common, hf_scr, yf_ref),
              _ssd_direction(1, actb, dtb, dttb, *common, hb_scr, yb_ref)]
    while stages:
        stages = [g for g in stages if next(g, StopIteration) is not StopIteration]


def _ssd(act, dt, dtt, dtb_row, dtb_col, alog_row, alog_col, dskip, expand):
    b, s, _ = act.shape
    nc = s // CHUNK
    fwd = lambda i: i
    bwd = lambda i: nc - 1 - i
    act_spec = lambda c: pl.BlockSpec((None, CHUNK, CONV_CH), lambda bi, i: (bi, c(i), 0))
    dt_spec = lambda c: pl.BlockSpec((None, CHUNK, DT_PAD), lambda bi, i: (bi, c(i), 0))
    dtt_spec = lambda c: pl.BlockSpec((None, 2 * SSD_HEADS, CHUNK), lambda bi, i: (bi, 0, c(i)))
    y_spec = lambda c: pl.BlockSpec((None, CHUNK, SSD_D_INNER), lambda bi, i: (bi, c(i), 0))
    full = lambda a: pl.BlockSpec(a.shape, lambda bi, i: (0,) * a.ndim)
    consts = (dtb_row, dtb_col, alog_row, alog_col, dskip, expand)
    y_shape = jax.ShapeDtypeStruct((b, s, SSD_D_INNER), BF16)
    return pl.pallas_call(
        _ssd_kernel,
        grid=(b, nc),
        in_specs=[act_spec(fwd), act_spec(bwd), dt_spec(fwd), dtt_spec(fwd), dt_spec(bwd),
                  dtt_spec(bwd)] + [full(a) for a in consts],
        out_specs=(y_spec(fwd), y_spec(bwd)),
        out_shape=(y_shape, y_shape),
        scratch_shapes=[pltpu.VMEM((SSD_GROUPS, SSD_STATE, GROUP_WIDTH), F32),
                        pltpu.VMEM((SSD_GROUPS, SSD_STATE, GROUP_WIDTH), F32)],
        compiler_params=pltpu.CompilerParams(dimension_semantics=("arbitrary", "arbitrary"),
                                             vmem_limit_bytes=VMEM_LIMIT),
        name="ssd",
    )(act, act, dt, dtt, dt, dtt, *consts)


def _merge_kernel(x_ref, yf_ref, yb_ref, z_ref, ot_ref, gates_ref, wssd_ref, wattn_ref, wout_ref,
                  gssd_ref, gpost_ref, o_ref):
    rows = x_ref.shape[0] // MERGE_SPLIT

    def slab(i):
        rs = slice(i * rows, (i + 1) * rows)
        attn_out = lax.dot_general(ot_ref[:, rs], wattn_ref[...], (((0,), (0,)), ((), ())),
                                   preferred_element_type=F32)
        yield
        y = yf_ref[rs, :].astype(F32) + yb_ref[rs, :].astype(F32)
        z = z_ref[rs, :].astype(F32)
        ssd_in = _rms(y * (z * jax.nn.sigmoid(z)), gssd_ref[...]).astype(BF16)
        ssd_out = _dot(ssd_in, wssd_ref[...])
        yield
        gates = jax.nn.sigmoid(gates_ref[rs, :].astype(F32))
        mixed = gates[:, :D_MODEL] * attn_out + gates[:, D_MODEL:] * ssd_out
        proj = _dot(mixed.astype(BF16), wout_ref[...])
        yield
        o_ref[rs, :] = x_ref[rs, :] + _rms(proj, gpost_ref[...])

    stages = [slab(i) for i in range(MERGE_SPLIT)]
    while stages:
        stages = [g for g in stages if next(g, StopIteration) is not StopIteration]


def _merge(x2, yf, yb, z, ot, gates, wssd, wattn, wout, gssd, gpost, seq):
    n_tok = x2.shape[0]
    tm = TOK_BLOCK
    bps = seq // tm
    full = lambda a: pl.BlockSpec(a.shape, lambda i: (0,) * a.ndim)
    row = lambda w: pl.BlockSpec((tm, w), lambda i: (i, 0))
    return pl.pallas_call(
        _merge_kernel,
        grid=(n_tok // tm,),
        in_specs=[row(D_MODEL), row(SSD_D_INNER), row(SSD_D_INNER), row(SSD_D_INNER),
                  pl.BlockSpec((None, ATTN_WIDTH, tm), lambda i: (i // bps, 0, i % bps)),
                  row(2 * D_MODEL), full(wssd), full(wattn), full(wout), full(gssd), full(gpost)],
        out_specs=row(D_MODEL),
        out_shape=jax.ShapeDtypeStruct((n_tok, D_MODEL), F32),
        compiler_params=pltpu.CompilerParams(dimension_semantics=("arbitrary",),
                                             vmem_limit_bytes=VMEM_LIMIT),
        name="merge",
    )(x2, yf, yb, z, ot, gates, wssd, wattn, wout, gssd, gpost)


def _ffn_kernel(x_ref, gpre_ref, wgate_ref, wup_ref, wdown_ref, gpost_ref, o_ref):
    x = x_ref[...]
    h = _rms(x, gpre_ref[...]).astype(BF16)
    gate = _dot(h, wgate_ref[...])
    up = _dot(h, wup_ref[...])
    act = (gate * jax.nn.sigmoid(gate) * up).astype(BF16)
    ffn = _dot(act, wdown_ref[...])
    o_ref[...] = x + _rms(ffn, gpost_ref[...])


def _ffn(x2, gpre, wgate, wup, wdown, gpost):
    n_tok = x2.shape[0]
    tm = TOK_BLOCK
    full = lambda a: pl.BlockSpec(a.shape, lambda i: (0,) * a.ndim)
    row = pl.BlockSpec((tm, D_MODEL), lambda i: (i, 0))
    return pl.pallas_call(
        _ffn_kernel,
        grid=(n_tok // tm,),
        in_specs=[row, full(gpre), full(wgate), full(wup), full(wdown), full(gpost)],
        out_specs=row,
        out_shape=jax.ShapeDtypeStruct((n_tok, D_MODEL), F32),
        compiler_params=pltpu.CompilerParams(dimension_semantics=("arbitrary",),
                                             vmem_limit_bytes=VMEM_LIMIT),
        name="ffn",
    )(x2, gpre, wgate, wup, wdown, gpost)


def _rope_tables(seq):
    rows = seq // GRID_W
    row_idx = jnp.repeat(jnp.arange(rows, dtype=F32), GRID_W)
    col_idx = jnp.tile(jnp.arange(GRID_W, dtype=F32), rows)
    inv_freq = ROPE_THETA ** (-jnp.arange(0, AXIAL_DIM, 2, dtype=F32) / AXIAL_DIM)
    ang_row = row_idx[:, None] * inv_freq[None, :]
    ang_col = col_idx[:, None] * inv_freq[None, :]
    cr, sr, cc, sc = jnp.cos(ang_row), jnp.sin(ang_row), jnp.cos(ang_col), jnp.sin(ang_col)
    cos64 = jnp.concatenate([cr, cr, cc, cc], axis=-1)
    sin64 = jnp.concatenate([-sr, sr, -sc, sc], axis=-1)
    return jnp.tile(cos64, (1, 2)), jnp.tile(sin64, (1, 2))


def kernel(x, w_in, q_norm, k_norm, conv_w, conv_b, dt_bias_f, dt_bias_b, a_log_f, a_log_b, d_skip, ssd_norm, w_attn_proj, w_ssd_proj, w_out, norm1_pre, norm1_post, norm2_pre, norm2_post, w_gate_up, w_down):
    b, s, _ = x.shape
    assert w_in.shape[0] == 1, "single-layer block"
    assert s % TOK_BLOCK == 0 and s % Q_BLOCK == 0 and s % KV_BLOCK == 0 and s % CHUNK == 0
    x2 = x.reshape(b * s, D_MODEL)

    w = w_in[0]
    o_z = QK_WIDTH + KV_WIDTH
    o_xbc = o_z + SSD_D_INNER
    o_dt = o_xbc + CONV_CH
    o_g = o_dt + 2 * SSD_HEADS
    wqkv = w[:, :o_z].astype(BF16)
    wz = w[:, o_z:o_xbc].astype(BF16)
    wxbc = w[:, o_xbc:o_dt].astype(BF16)
    wdt = jnp.pad(w[:, o_dt:o_g], ((0, 0), (0, DT_PAD - 2 * SSD_HEADS))).astype(BF16)
    wg = w[:, o_g:].astype(BF16)
    row = lambda v: v.reshape(1, -1).astype(F32)
    qkg = jnp.concatenate([jnp.tile(q_norm[0], N_Q_HEADS), jnp.tile(k_norm[0], N_KV_HEADS)]).reshape(1, -1)
    cos_t, sin_t = _rope_tables(s)
    seg = np.arange(128) // HEAD_DIM
    bd = jnp.asarray(seg[:, None] == seg[None, :], dtype=BF16)

    convw = jnp.pad(conv_w[0].astype(F32), ((0, 8 - CONV_K), (0, 0)))
    qt, k, vt, z, act, gates, dt, dtt = _in_proj(x2, row(norm1_pre[0]), wqkv, wz, wxbc, wg, wdt,
                                                 qkg, cos_t, sin_t, bd, convw, row(conv_b[0]), b, s)
    ot = _attention(qt, k.reshape(b, s, KV_WIDTH), vt)

    dt3 = dt.reshape(b, s, DT_PAD)
    dt_bias = jnp.concatenate([dt_bias_f[0], dt_bias_b[0]]).astype(F32)
    a_log = jnp.concatenate([a_log_f[0], a_log_b[0]]).astype(F32)
    dskip = jnp.repeat(d_skip[0].astype(F32), SSD_HEADDIM).reshape(1, -1)
    hid = np.arange(SSD_D_INNER) // SSD_HEADDIM
    expand = jnp.asarray(np.arange(SSD_HEADS)[:, None] == hid[None, :], dtype=BF16)
    yf, yb = _ssd(act.reshape(b, s, CONV_CH), dt3, dtt, dt_bias.reshape(1, -1), dt_bias.reshape(-1, 1),
                  a_log.reshape(1, -1), a_log.reshape(-1, 1), dskip, expand)

    x1 = _merge(x2, yf.reshape(b * s, -1), yb.reshape(b * s, -1), z, ot, gates,
                w_ssd_proj[0].astype(BF16), w_attn_proj[0].astype(BF16), w_out[0].astype(BF16),
                row(ssd_norm[0]), row(norm1_post[0]), s)

    wgu = w_gate_up[0]
    out = _ffn(x1, row(norm2_pre[0]), wgu[:, :D_FF].astype(BF16), wgu[:, D_FF:].astype(BF16),
               w_down[0].astype(BF16), row(norm2_post[0]))
    return out.reshape(b, s, D_MODEL)
```

```python
import functools
import math

import jax
import jax.numpy as jnp
import numpy as np
from jax import lax
from jax.experimental import pallas as pl
from jax.experimental.pallas import tpu as pltpu

F32 = jnp.float32
BF16 = jnp.bfloat16

D_MODEL = 1024
GRID_W = 64
HEAD_DIM = 64
N_Q_HEADS = 8
N_KV_HEADS = 2
Q_PER_KV = N_Q_HEADS // N_KV_HEADS
ATTN_WIDTH = N_Q_HEADS * HEAD_DIM
KV_WIDTH = N_KV_HEADS * HEAD_DIM
QK_WIDTH = ATTN_WIDTH + KV_WIDTH
ROPE_THETA = 10000.0
AXIAL_DIM = HEAD_DIM // 2
SSD_D_INNER = D_MODEL
SSD_HEADDIM = 64
SSD_HEADS = SSD_D_INNER // SSD_HEADDIM
SSD_GROUPS = 2
HEADS_PER_GROUP = SSD_HEADS // SSD_GROUPS
GROUP_WIDTH = HEADS_PER_GROUP * SSD_HEADDIM
SSD_STATE = 128
CONV_K = 5
CONV_PAD = (CONV_K - 1) // 2
CHUNK = 128
CONV_CH = SSD_D_INNER + 2 * SSD_GROUPS * SSD_STATE
D_FF = ((8 * D_MODEL // 3 + 255) // 256) * 256
EPS = 1e-6
DT_PAD = 128
X_HALO = 8

VMEM_LIMIT = 56 * 1024 * 1024

TOK_BLOCK = 512
Q_BLOCK = 512
KV_BLOCK = 256
Q_TILE = 512
KV_UNROLL = 32
ONES_ROWS = 16
V_ROWS = HEAD_DIM + ONES_ROWS


def _rms(xf, gain):
    return xf * lax.rsqrt(jnp.mean(xf * xf, axis=-1, keepdims=True) + EPS) * gain


def _split_bf16(a, parts):
    out = []
    r = a
    for _ in range(parts):
        p = r.astype(BF16)
        out.append(p)
        r = r - p.astype(F32)
    return out


def _dot(a, b):
    return jnp.dot(a, b, preferred_element_type=F32)


def _in_proj_kernel(x_ref, xprev_ref, xnext_ref, g_ref, wqkv_ref, wz_ref, wxbc_ref, wg_ref, wdt_ref,
                    qkg_ref, cos_ref, sin_ref, bd_ref, convw_ref, convb_ref, qt_ref, k_ref, vt_ref,
                    z_ref, act_ref, gates_ref, dt_ref, dtt_ref, ext_scr, *, blocks_per_seq):
    tm = x_ref.shape[0]
    h = _rms(x_ref[...], g_ref[...]).astype(BF16)

    pos = pl.program_id(0) % blocks_per_seq
    x_halo = jnp.concatenate([xprev_ref[...], xnext_ref[...]], axis=0)
    halo = _dot(_rms(x_halo, g_ref[...]).astype(BF16), wxbc_ref[...])
    ext_scr[0:X_HALO, :] = halo[:X_HALO] * (pos > 0).astype(F32)
    ext_scr[X_HALO:X_HALO + tm, :] = _dot(h, wxbc_ref[...])
    ext_scr[X_HALO + tm:, :] = halo[X_HALO:] * (pos < blocks_per_seq - 1).astype(F32)

    z_ref[...] = _dot(h, wz_ref[...]).astype(BF16)
    gates_ref[...] = _dot(h, wg_ref[...]).astype(BF16)
    dt = _dot(h, wdt_ref[...])
    dt_ref[...] = dt
    dtt_ref[...] = dt.T[:2 * SSD_HEADS, :]
    qkv = _dot(h, wqkv_ref[...])
    vt = qkv[:, QK_WIDTH:].T
    for hd in range(N_KV_HEADS):
        vt_ref[hd, 0:HEAD_DIM, :] = vt[hd * HEAD_DIM:(hd + 1) * HEAD_DIM, :].astype(BF16)
        vt_ref[hd, HEAD_DIM:, :] = jnp.ones((ONES_ROWS, vt.shape[1]), BF16)

    qk = qkv[:, :QK_WIDTH]
    sq = (qk * qk).astype(BF16)
    bd = bd_ref[...]
    n_lane_tiles = QK_WIDTH // 128
    ms = jnp.concatenate([_dot(sq[:, i * 128:(i + 1) * 128], bd) for i in range(n_lane_tiles)],
                         axis=1) * (1.0 / HEAD_DIM)
    qn = qk * lax.rsqrt(ms + EPS) * qkg_ref[...]
    half = AXIAL_DIM // 2
    lane = lax.broadcasted_iota(jnp.int32, qn.shape, 1)
    partner = jnp.where((lane % AXIAL_DIM) < half,
                        pltpu.roll(qn, QK_WIDTH - half, 1), pltpu.roll(qn, half, 1))
    cos = jnp.concatenate([cos_ref[...]] * n_lane_tiles, axis=1)
    sin = jnp.concatenate([sin_ref[...]] * n_lane_tiles, axis=1)
    roped = qn * cos + partner * sin
    q = roped[:, :ATTN_WIDTH] * (math.log2(math.e) / math.sqrt(HEAD_DIM))
    qt_ref[...] = q.T.astype(BF16)
    k_ref[...] = roped[:, ATTN_WIDTH:].astype(BF16)

    conv = jnp.broadcast_to(convb_ref[...], (tm, CONV_CH))
    for k in range(CONV_K):
        conv = conv + convw_ref[k:k + 1, :] * ext_scr[pl.ds(X_HALO - CONV_PAD + k, tm), :]
    act_ref[...] = (conv * jax.nn.sigmoid(conv)).astype(BF16)


def _in_proj(x2, g, wqkv, wz, wxbc, wg, wdt, qkg, cos_t, sin_t, bd, convw, convb, batch, seq):
    n_tok = x2.shape[0]
    tm = TOK_BLOCK
    blocks_per_seq = seq // tm
    hpb = tm // X_HALO
    last_halo = n_tok // X_HALO - 1
    xprev = pl.BlockSpec((X_HALO, D_MODEL), lambda i: (jnp.maximum(i * hpb - 1, 0), 0))
    xnext = pl.BlockSpec((X_HALO, D_MODEL), lambda i: (jnp.minimum((i + 1) * hpb, last_halo), 0))
    full = lambda a: pl.BlockSpec(a.shape, lambda i: (0,) * a.ndim)
    row = lambda w: pl.BlockSpec((tm, w), lambda i: (i, 0))
    tab = pl.BlockSpec((tm, 128), lambda i: (i % blocks_per_seq, 0))
    col = lambda r: pl.BlockSpec((None, r, tm),
                                 lambda i: (i // blocks_per_seq, 0, i % blocks_per_seq))
    vt_spec = pl.BlockSpec((None, N_KV_HEADS, V_ROWS, tm),
                           lambda i: (i // blocks_per_seq, 0, 0, i % blocks_per_seq))
    out_shapes = (
        jax.ShapeDtypeStruct((batch, ATTN_WIDTH, seq), BF16),
        jax.ShapeDtypeStruct((n_tok, KV_WIDTH), BF16),
        jax.ShapeDtypeStruct((batch, N_KV_HEADS, V_ROWS, seq), BF16),
        jax.ShapeDtypeStruct((n_tok, SSD_D_INNER), BF16),
        jax.ShapeDtypeStruct((n_tok, CONV_CH), BF16),
        jax.ShapeDtypeStruct((n_tok, 2 * D_MODEL), BF16),
        jax.ShapeDtypeStruct((n_tok, DT_PAD), F32),
        jax.ShapeDtypeStruct((batch, 2 * SSD_HEADS, seq), F32),
    )
    return pl.pallas_call(
        functools.partial(_in_proj_kernel, blocks_per_seq=blocks_per_seq),
        grid=(n_tok // tm,),
        in_specs=[row(D_MODEL), xprev, xnext, full(g), full(wqkv), full(wz), full(wxbc), full(wg),
                  full(wdt), full(qkg), tab, tab, full(bd), full(convw), full(convb)],
        out_specs=(col(ATTN_WIDTH), row(KV_WIDTH), vt_spec, row(SSD_D_INNER), row(CONV_CH),
                   row(2 * D_MODEL), row(DT_PAD), col(2 * SSD_HEADS)),
        out_shape=out_shapes,
        scratch_shapes=[pltpu.VMEM((tm + 2 * X_HALO, CONV_CH), F32)],
        compiler_params=pltpu.CompilerParams(dimension_semantics=("arbitrary",),
                                             vmem_limit_bytes=VMEM_LIMIT),
        name="in_proj",
    )(x2, x2, x2, g, wqkv, wz, wxbc, wg, wdt, qkg, cos_t, sin_t, bd, convw, convb)


def _sublane_allreduce(x, op):
    for shift in (4, 2, 1):
        x = op(x, pltpu.roll(x, shift, 0))
    return x


def _attn_kernel(qt_ref, k_ref, vt_ref, o_ref, qpad_scr, m_scr, acc_scr, s_scr, cmax_scr, *,
                 n_kv_blocks):
    kvh = pl.program_id(1)
    bq = qt_ref.shape[1]
    n_tiles = (Q_PER_KV * bq) // Q_TILE
    q_tile = jnp.concatenate(
        [qt_ref[g * HEAD_DIM:(g + 1) * HEAD_DIM, :] for g in range(Q_PER_KV)], axis=1)
    zero = jnp.zeros_like(q_tile)
    qpad_scr[0:HEAD_DIM, :] = jnp.where(kvh == 0, q_tile, zero)
    qpad_scr[HEAD_DIM:, :] = jnp.where(kvh == 0, zero, q_tile)
    m_scr[...] = jnp.full(m_scr.shape, -1e30, F32)
    acc_scr[...] = jnp.zeros(acc_scr.shape, F32)

    def scores(start, t):
        return _dot(k_ref[pl.ds(start, KV_BLOCK), :], qpad_scr[:, t * Q_TILE:(t + 1) * Q_TILE])

    def colmax(s):
        return _sublane_allreduce(jnp.max(s.reshape(KV_BLOCK // 8, 8, Q_TILE), axis=0), jnp.maximum)

    def softmax_pv(t, s, cmax, vt_blk):
        cols = slice(t * Q_TILE, (t + 1) * Q_TILE)
        m_prev = m_scr[:, cols]
        m_new = jnp.maximum(m_prev, cmax)
        alpha = jnp.exp2(m_prev - m_new)
        p = jnp.exp2(s.reshape(KV_BLOCK // 8, 8, Q_TILE) - m_new[None])
        pv = _dot(vt_blk, p.reshape(KV_BLOCK, Q_TILE).astype(BF16))
        acc = acc_scr[:, cols].reshape(V_ROWS // 8, 8, Q_TILE) * alpha[None]
        acc_scr[:, cols] = acc.reshape(V_ROWS, Q_TILE) + pv
        m_scr[:, cols] = m_new

    assert n_tiles >= 2
    s_scr[0] = scores(0, 0)
    s_scr[1] = scores(0, 1)
    cmax_scr[...] = colmax(s_scr[0])

    def body(j, carry):
        start = pl.multiple_of(j * KV_BLOCK, KV_BLOCK)
        nxt = pl.multiple_of(jnp.minimum(j + 1, n_kv_blocks - 1) * KV_BLOCK, KV_BLOCK)
        vt_blk = vt_ref[:, pl.ds(start, KV_BLOCK)]
        s_cur, s_nxt, cmax_cur = s_scr[0], s_scr[1], cmax_scr[...]
        for t in range(n_tiles):
            ahead = t + 2
            s_new = scores(start, ahead) if ahead < n_tiles else scores(nxt, ahead - n_tiles)
            cmax_nxt = colmax(s_nxt)
            softmax_pv(t, s_cur, cmax_cur, vt_blk)
            s_cur, s_nxt, cmax_cur = s_nxt, s_new, cmax_nxt
        s_scr[0] = s_cur
        s_scr[1] = s_nxt
        cmax_scr[...] = cmax_cur
        return carry

    lax.fori_loop(0, n_kv_blocks, body, 0, unroll=KV_UNROLL)
    inv_l = 1.0 / acc_scr[HEAD_DIM:HEAD_DIM + 8, :]
    out = acc_scr[0:HEAD_DIM, :].reshape(HEAD_DIM // 8, 8, Q_PER_KV * bq) * inv_l[None]
    out = out.reshape(HEAD_DIM, Q_PER_KV * bq)
    for g in range(Q_PER_KV):
        o_ref[g * HEAD_DIM:(g + 1) * HEAD_DIM, :] = out[:, g * bq:(g + 1) * bq].astype(BF16)


def _attention(qt, k, vt):
    b, _, s = qt.shape
    bq = Q_BLOCK
    rows = Q_PER_KV * HEAD_DIM
    return pl.pallas_call(
        functools.partial(_attn_kernel, n_kv_blocks=s // KV_BLOCK),
        grid=(b, N_KV_HEADS, s // bq),
        in_specs=[pl.BlockSpec((None, rows, bq), lambda bi, h, qi: (bi, h, qi)),
                  pl.BlockSpec((None, s, KV_WIDTH), lambda bi, h, qi: (bi, 0, 0)),
                  pl.BlockSpec((None, None, V_ROWS, s), lambda bi, h, qi: (bi, h, 0, 0))],
        out_specs=pl.BlockSpec((None, rows, bq), lambda bi, h, qi: (bi, h, qi)),
        out_shape=jax.ShapeDtypeStruct((b, ATTN_WIDTH, s), BF16),
        scratch_shapes=[pltpu.VMEM((KV_WIDTH, Q_PER_KV * bq), BF16),
                        pltpu.VMEM((8, Q_PER_KV * bq), F32),
                        pltpu.VMEM((V_ROWS, Q_PER_KV * bq), F32),
                        pltpu.VMEM((2, KV_BLOCK, Q_TILE), F32),
                        pltpu.VMEM((8, Q_TILE), F32)],
        compiler_params=pltpu.CompilerParams(
            dimension_semantics=("arbitrary", "arbitrary", "arbitrary"),
            vmem_limit_bytes=VMEM_LIMIT),
        name="attention",
    )(qt, k, vt)


def _softplus(x):
    return jnp.maximum(x, 0.0) + jnp.log1p(jnp.exp(-jnp.abs(x)))


def _ssd_direction(d, act_ref, dt_ref, dtt_ref, dtb_row_ref, dtb_col_ref, alog_row_ref,
                   alog_col_ref, dskip_ref, expand_ref, h_scr, y_ref):
    hs = slice(d * SSD_HEADS, (d + 1) * SSD_HEADS)
    c_off = SSD_D_INNER + SSD_GROUPS * SSD_STATE
    b_gs = [act_ref[:, SSD_D_INNER + g * SSD_STATE:SSD_D_INNER + (g + 1) * SSD_STATE]
            for g in range(SSD_GROUPS)]
    c_gs = [act_ref[:, c_off + g * SSD_STATE:c_off + (g + 1) * SSD_STATE]
            for g in range(SSD_GROUPS)]

    cbs = [lax.dot_general(c_gs[g], b_gs[g], (((1,), (1,)), ((), ())), preferred_element_type=F32)
           for g in range(SSD_GROUPS)]
    h_ins = [h_scr[g] for g in range(SSD_GROUPS)]
    y_offs = [_dot(c_gs[g], h_ins[g].astype(BF16)) for g in range(SSD_GROUPS)]
    yield

    dt_c = _softplus(dt_ref[:, hs] + dtb_row_ref[:, hs])
    dt_r = _softplus(dtt_ref[hs, :] + dtb_col_ref[hs, :])
    adt_c = dt_c * (-jnp.exp(alog_row_ref[:, hs]))
    adt_r = dt_r * (-jnp.exp(alog_col_ref[hs, :]))
    ri = lax.broadcasted_iota(jnp.int32, (CHUNK, CHUNK), 0)
    ci = lax.broadcasted_iota(jnp.int32, (CHUNK, CHUNK), 1)
    lower = ri >= ci
    upper = ri <= ci
    before = upper if d else lower
    tri = jnp.where(before, 1.0, 0.0).astype(BF16)
    tri_t = jnp.where(lower if d else upper, 1.0, 0.0).astype(BF16)
    c_col = sum(_dot(tri, p) for p in _split_bf16(adt_c, 3))
    c_row = sum(_dot(p, tri_t) for p in _split_bf16(adt_r, 3))
    end = 0 if d else CHUNK - 1
    c_end = c_col[end:end + 1, :]
    yield

    expand = expand_ref[...]
    dt_x = _dot(dt_c.astype(BF16), expand)
    ec_x = _dot(jnp.exp(c_col).astype(BF16), expand)
    ed_x = _dot(jnp.exp(c_end - c_col).astype(BF16), expand)
    eend = jnp.broadcast_to(jnp.exp(c_end), (8, SSD_HEADS))
    eend_x = sum(_dot(p, expand) for p in _split_bf16(eend, 3))[0:1, :]
    xs = act_ref[:, :SSD_D_INNER].astype(F32)
    xd = xs * dt_x
    xd_b = xd.astype(BF16)
    xde_b = (xd * ed_x).astype(BF16)
    yield

    pair_w = 2 * SSD_HEADDIM
    lane = lax.broadcasted_iota(jnp.int32, (CHUNK, pair_w), 1)
    zero_b = jnp.zeros((CHUNK, pair_w), BF16)
    y_pairs = []
    for hd0 in range(0, SSD_HEADS, 2):
        cb = cbs[hd0 // HEADS_PER_GROUP]
        ws = []
        for hd in (hd0, hd0 + 1):
            seg = c_col[:, hd:hd + 1] - c_row[hd:hd + 1, :]
            ws.append((cb * jnp.where(before, jnp.exp(seg), 0.0)).astype(BF16))
        xd_pair = xd_b[:, hd0 * SSD_HEADDIM:hd0 * SSD_HEADDIM + pair_w]
        diag = jnp.concatenate([jnp.where(lane < SSD_HEADDIM, xd_pair, zero_b),
                                jnp.where(lane < SSD_HEADDIM, zero_b, xd_pair)], axis=0)
        y_pairs.append(_dot(jnp.concatenate(ws, axis=1), diag))
        yield

    for g in range(SSD_GROUPS):
        gs = slice(g * GROUP_WIDTH, (g + 1) * GROUP_WIDTH)
        upd = lax.dot_general(b_gs[g], xde_b[:, gs], (((0,), (0,)), ((), ())),
                              preferred_element_type=F32)
        h_scr[g] = h_ins[g] * eend_x[:, gs] + upd
    y = jnp.concatenate(y_pairs, axis=1) + jnp.concatenate(y_offs, axis=1) * ec_x
    if d == 0:
        y = y + dskip_ref[...] * xs
    y_ref[...] = y.astype(BF16)


def _ssd_kernel(actf, actb, dtf, dttf, dtb, dttb, dtb_row, dtb_col, alog_row, alog_col, dskip,
                expand, yf_ref, yb_ref, hf_scr, hb_scr):
    @pl.when(pl.program_id(1) == 0)
    def _():
        hf_scr[...] = jnp.zeros(hf_scr.shape, F32)
        hb_scr[...] = jnp.zeros(hb_scr.shape, F32)

    common = (dtb_row, dtb_col, alog_row, alog_col, dskip, expand)
    stages = [_ssd_direction(0, actf, dtf, dttf, *common, hf_scr, yf_ref),
              _ssd_direction(1, actb, dtb, dttb, *common, hb_scr, yb_ref)]
    while stages:
        stages = [g for g in stages if next(g, StopIteration) is not StopIteration]


def _ssd(act, dt, dtt, dtb_row, dtb_col, alog_row, alog_col, dskip, expand):
    b, s, _ = act.shape
    nc = s // CHUNK
    fwd = lambda i: i
    bwd = lambda i: nc - 1 - i
    act_spec = lambda c: pl.BlockSpec((None, CHUNK, CONV_CH), lambda bi, i: (bi, c(i), 0))
    dt_spec = lambda c: pl.BlockSpec((None, CHUNK, DT_PAD), lambda bi, i: (bi, c(i), 0))
    dtt_spec = lambda c: pl.BlockSpec((None, 2 * SSD_HEADS, CHUNK), lambda bi, i: (bi, 0, c(i)))
    y_spec = lambda c: pl.BlockSpec((None, CHUNK, SSD_D_INNER), lambda bi, i: (bi, c(i), 0))
    full = lambda a: pl.BlockSpec(a.shape, lambda bi, i: (0,) * a.ndim)
    consts = (dtb_row, dtb_col, alog_row, alog_col, dskip, expand)
    y_shape = jax.ShapeDtypeStruct((b, s, SSD_D_INNER), BF16)
    return pl.pallas_call(
        _ssd_kernel,
        grid=(b, nc),
        in_specs=[act_spec(fwd), act_spec(bwd), dt_spec(fwd), dtt_spec(fwd), dt_spec(bwd),
                  dtt_spec(bwd)] + [full(a) for a in consts],
        out_specs=(y_spec(fwd), y_spec(bwd)),
        out_shape=(y_shape, y_shape),
        scratch_shapes=[pltpu.VMEM((SSD_GROUPS, SSD_STATE, GROUP_WIDTH), F32),
                        pltpu.VMEM((SSD_GROUPS, SSD_STATE, GROUP_WIDTH), F32)],
        compiler_params=pltpu.CompilerParams(dimension_semantics=("arbitrary", "arbitrary"),
                                             vmem_limit_bytes=VMEM_LIMIT),
        name="ssd",
    )(act, act, dt, dtt, dt, dtt, *consts)


def _merge_kernel(x_ref, yf_ref, yb_ref, z_ref, ot_ref, gates_ref, wssd_ref, wattn_ref, wout_ref,
                  gssd_ref, gpost_ref, o_ref):
    y = yf_ref[...].astype(F32) + yb_ref[...].astype(F32)
    z = z_ref[...].astype(F32)
    ssd_in = _rms(y * (z * jax.nn.sigmoid(z)), gssd_ref[...]).astype(BF16)
    ssd_out = _dot(ssd_in, wssd_ref[...])
    attn_out = lax.dot_general(ot_ref[...], wattn_ref[...], (((0,), (0,)), ((), ())),
                               preferred_element_type=F32)
    gates = jax.nn.sigmoid(gates_ref[...].astype(F32))
    mixed = gates[:, :D_MODEL] * attn_out + gates[:, D_MODEL:] * ssd_out
    proj = _dot(mixed.astype(BF16), wout_ref[...])
    o_ref[...] = x_ref[...] + _rms(proj, gpost_ref[...])


def _merge(x2, yf, yb, z, ot, gates, wssd, wattn, wout, gssd, gpost, seq):
    n_tok = x2.shape[0]
    tm = TOK_BLOCK
    bps = seq // tm
    full = lambda a: pl.BlockSpec(a.shape, lambda i: (0,) * a.ndim)
    row = lambda w: pl.BlockSpec((tm, w), lambda i: (i, 0))
    return pl.pallas_call(
        _merge_kernel,
        grid=(n_tok // tm,),
        in_specs=[row(D_MODEL), row(SSD_D_INNER), row(SSD_D_INNER), row(SSD_D_INNER),
                  pl.BlockSpec((None, ATTN_WIDTH, tm), lambda i: (i // bps, 0, i % bps)),
                  row(2 * D_MODEL), full(wssd), full(wattn), full(wout), full(gssd), full(gpost)],
        out_specs=row(D_MODEL),
        out_shape=jax.ShapeDtypeStruct((n_tok, D_MODEL), F32),
        compiler_params=pltpu.CompilerParams(dimension_semantics=("arbitrary",),
                                             vmem_limit_bytes=VMEM_LIMIT),
        name="merge",
    )(x2, yf, yb, z, ot, gates, wssd, wattn, wout, gssd, gpost)


def _ffn_kernel(x_ref, gpre_ref, wgate_ref, wup_ref, wdown_ref, gpost_ref, o_ref):
    x = x_ref[...]
    h = _rms(x, gpre_ref[...]).astype(BF16)
    gate = _dot(h, wgate_ref[...])
    up = _dot(h, wup_ref[...])
    act = (gate * jax.nn.sigmoid(gate) * up).astype(BF16)
    ffn = _dot(act, wdown_ref[...])
    o_ref[...] = x + _rms(ffn, gpost_ref[...])


def _ffn(x2, gpre, wgate, wup, wdown, gpost):
    n_tok = x2.shape[0]
    tm = TOK_BLOCK
    full = lambda a: pl.BlockSpec(a.shape, lambda i: (0,) * a.ndim)
    row = pl.BlockSpec((tm, D_MODEL), lambda i: (i, 0))
    return pl.pallas_call(
        _ffn_kernel,
        grid=(n_tok // tm,),
        in_specs=[row, full(gpre), full(wgate), full(wup), full(wdown), full(gpost)],
        out_specs=row,
        out_shape=jax.ShapeDtypeStruct((n_tok, D_MODEL), F32),
        compiler_params=pltpu.CompilerParams(dimension_semantics=("arbitrary",),
                                             vmem_limit_bytes=VMEM_LIMIT),
        name="ffn",
    )(x2, gpre, wgate, wup, wdown, gpost)


def _rope_tables(seq):
    rows = seq // GRID_W
    row_idx = jnp.repeat(jnp.arange(rows, dtype=F32), GRID_W)
    col_idx = jnp.tile(jnp.arange(GRID_W, dtype=F32), rows)
    inv_freq = ROPE_THETA ** (-jnp.arange(0, AXIAL_DIM, 2, dtype=F32) / AXIAL_DIM)
    ang_row = row_idx[:, None] * inv_freq[None, :]
    ang_col = col_idx[:, None] * inv_freq[None, :]
    cr, sr, cc, sc = jnp.cos(ang_row), jnp.sin(ang_row), jnp.cos(ang_col), jnp.sin(ang_col)
    cos64 = jnp.concatenate([cr, cr, cc, cc], axis=-1)
    sin64 = jnp.concatenate([-sr, sr, -sc, sc], axis=-1)
    return jnp.tile(cos64, (1, 2)), jnp.tile(sin64, (1, 2))


def kernel(x, w_in, q_norm, k_norm, conv_w, conv_b, dt_bias_f, dt_bias_b, a_log_f, a_log_b, d_skip, ssd_norm, w_attn_proj, w_ssd_proj, w_out, norm1_pre, norm1_post, norm2_pre, norm2_post, w_gate_up, w_down):
    b, s, _ = x.shape
    assert w_in.shape[0] == 1, "single-layer block"
    assert s % TOK_BLOCK == 0 and s % Q_BLOCK == 0 and s % KV_BLOCK == 0 and s % CHUNK == 0
    x2 = x.reshape(b * s, D_MODEL)

    w = w_in[0]
    o_z = QK_WIDTH + KV_WIDTH
    o_xbc = o_z + SSD_D_INNER
    o_dt = o_xbc + CONV_CH
    o_g = o_dt + 2 * SSD_HEADS
    wqkv = w[:, :o_z].astype(BF16)
    wz = w[:, o_z:o_xbc].astype(BF16)
    wxbc = w[:, o_xbc:o_dt].astype(BF16)
    wdt = jnp.pad(w[:, o_dt:o_g], ((0, 0), (0, DT_PAD - 2 * SSD_HEADS))).astype(BF16)
    wg = w[:, o_g:].astype(BF16)
    row = lambda v: v.reshape(1, -1).astype(F32)
    qkg = jnp.concatenate([jnp.tile(q_norm[0], N_Q_HEADS), jnp.tile(k_norm[0], N_KV_HEADS)]).reshape(1, -1)
    cos_t, sin_t = _rope_tables(s)
    seg = np.arange(128) // HEAD_DIM
    bd = jnp.asarray(seg[:, None] == seg[None, :], dtype=BF16)

    convw = jnp.pad(conv_w[0].astype(F32), ((0, 8 - CONV_K), (0, 0)))
    qt, k, vt, z, act, gates, dt, dtt = _in_proj(x2, row(norm1_pre[0]), wqkv, wz, wxbc, wg, wdt,
                                                 qkg, cos_t, sin_t, bd, convw, row(conv_b[0]), b, s)
    ot = _attention(qt, k.reshape(b, s, KV_WIDTH), vt)

    dt3 = dt.reshape(b, s, DT_PAD)
    dt_bias = jnp.concatenate([dt_bias_f[0], dt_bias_b[0]]).astype(F32)
    a_log = jnp.concatenate([a_log_f[0], a_log_b[0]]).astype(F32)
    dskip = jnp.repeat(d_skip[0].astype(F32), SSD_HEADDIM).reshape(1, -1)
    hid = np.arange(SSD_D_INNER) // SSD_HEADDIM
    expand = jnp.asarray(np.arange(SSD_HEADS)[:, None] == hid[None, :], dtype=BF16)
    yf, yb = _ssd(act.reshape(b, s, CONV_CH), dt3, dtt, dt_bias.reshape(1, -1), dt_bias.reshape(-1, 1),
                  a_log.reshape(1, -1), a_log.reshape(-1, 1), dskip, expand)

    x1 = _merge(x2, yf.reshape(b * s, -1), yb.reshape(b * s, -1), z, ot, gates,
                w_ssd_proj[0].astype(BF16), w_attn_proj[0].astype(BF16), w_out[0].astype(BF16),
                row(ssd_norm[0]), row(norm1_post[0]), s)

    wgu = w_gate_up[0]
    out = _ffn(x1, row(norm2_pre[0]), wgu[:, :D_FF].astype(BF16), wgu[:, D_FF:].astype(BF16),
               w_down[0].astype(BF16), row(norm2_post[0]))
    return out.reshape(b, s, D_MODEL)
```

```python
import functools
import math

import jax
import jax.numpy as jnp
import numpy as np
from jax import lax
from jax.experimental import pallas as pl
from jax.experimental.pallas import tpu as pltpu

F32 = jnp.float32
BF16 = jnp.bfloat16

D_MODEL = 1024
GRID_W = 64
HEAD_DIM = 64
N_Q_HEADS = 8
N_KV_HEADS = 2
Q_PER_KV = N_Q_HEADS // N_KV_HEADS
ATTN_WIDTH = N_Q_HEADS * HEAD_DIM
KV_WIDTH = N_KV_HEADS * HEAD_DIM
QK_WIDTH = ATTN_WIDTH + KV_WIDTH
ROPE_THETA = 10000.0
AXIAL_DIM = HEAD_DIM // 2
SSD_D_INNER = D_MODEL
SSD_HEADDIM = 64
SSD_HEADS = SSD_D_INNER // SSD_HEADDIM
SSD_GROUPS = 2
HEADS_PER_GROUP = SSD_HEADS // SSD_GROUPS
GROUP_WIDTH = HEADS_PER_GROUP * SSD_HEADDIM
SSD_STATE = 128
CONV_K = 5
CONV_PAD = (CONV_K - 1) // 2
CHUNK = 128
SSD_STEP_CHUNKS = 4
CONV_CH = SSD_D_INNER + 2 * SSD_GROUPS * SSD_STATE
D_FF = ((8 * D_MODEL // 3 + 255) // 256) * 256
EPS = 1e-6
DT_PAD = 128
X_HALO = 8

VMEM_LIMIT = 56 * 1024 * 1024

TOK_BLOCK = 512
Q_BLOCK = 512
KV_BLOCK = 256
Q_TILE = 512
KV_UNROLL = 64
ONES_ROWS = 16
V_ROWS = HEAD_DIM + ONES_ROWS


def _rms(xf, gain):
    return xf * lax.rsqrt(jnp.mean(xf * xf, axis=-1, keepdims=True) + EPS) * gain


def _split_bf16(a, parts):
    out = []
    r = a
    for _ in range(parts):
        p = r.astype(BF16)
        out.append(p)
        r = r - p.astype(F32)
    return out


def _dot(a, b):
    return jnp.dot(a, b, preferred_element_type=F32)


def _in_proj_kernel(x_ref, xprev_ref, xnext_ref, g_ref, wqkv_ref, wz_ref, wxbc_ref, wg_ref, wdt_ref,
                    qkg_ref, cos_ref, sin_ref, bd_ref, convw_ref, convb_ref, qt_ref, k_ref, vt_ref,
                    z_ref, act_ref, gates_ref, dt_ref, dtt_ref, ext_scr, *, blocks_per_seq):
    tm = x_ref.shape[0]
    h = _rms(x_ref[...], g_ref[...]).astype(BF16)

    pos = pl.program_id(0) % blocks_per_seq
    x_halo = jnp.concatenate([xprev_ref[...], xnext_ref[...]], axis=0)
    halo = _dot(_rms(x_halo, g_ref[...]).astype(BF16), wxbc_ref[...])
    ext_scr[0:X_HALO, :] = halo[:X_HALO] * (pos > 0).astype(F32)
    ext_scr[X_HALO:X_HALO + tm, :] = _dot(h, wxbc_ref[...])
    ext_scr[X_HALO + tm:, :] = halo[X_HALO:] * (pos < blocks_per_seq - 1).astype(F32)

    z_ref[...] = _dot(h, wz_ref[...]).astype(BF16)
    gates_ref[...] = _dot(h, wg_ref[...]).astype(BF16)
    dt = _dot(h, wdt_ref[...])
    dt_ref[...] = dt
    dtt_ref[...] = dt.T[:2 * SSD_HEADS, :]
    qkv = _dot(h, wqkv_ref[...])
    vt = qkv[:, QK_WIDTH:].T
    for hd in range(N_KV_HEADS):
        vt_ref[hd, 0:HEAD_DIM, :] = vt[hd * HEAD_DIM:(hd + 1) * HEAD_DIM, :].astype(BF16)
        vt_ref[hd, HEAD_DIM:, :] = jnp.ones((ONES_ROWS, vt.shape[1]), BF16)

    qk = qkv[:, :QK_WIDTH]
    sq = (qk * qk).astype(BF16)
    bd = bd_ref[...]
    n_lane_tiles = QK_WIDTH // 128
    ms = jnp.concatenate([_dot(sq[:, i * 128:(i + 1) * 128], bd) for i in range(n_lane_tiles)],
                         axis=1) * (1.0 / HEAD_DIM)
    qn = qk * lax.rsqrt(ms + EPS) * qkg_ref[...]
    half = AXIAL_DIM // 2
    lane = lax.broadcasted_iota(jnp.int32, qn.shape, 1)
    partner = jnp.where((lane % AXIAL_DIM) < half,
                        pltpu.roll(qn, QK_WIDTH - half, 1), pltpu.roll(qn, half, 1))
    cos = jnp.concatenate([cos_ref[...]] * n_lane_tiles, axis=1)
    sin = jnp.concatenate([sin_ref[...]] * n_lane_tiles, axis=1)
    roped = qn * cos + partner * sin
    q = roped[:, :ATTN_WIDTH] * (math.log2(math.e) / math.sqrt(HEAD_DIM))
    qt_ref[...] = q.T.astype(BF16)
    k_ref[...] = roped[:, ATTN_WIDTH:].astype(BF16)

    conv = jnp.broadcast_to(convb_ref[...], (tm, CONV_CH))
    for k in range(CONV_K):
        conv = conv + convw_ref[k:k + 1, :] * ext_scr[pl.ds(X_HALO - CONV_PAD + k, tm), :]
    act_ref[...] = (conv * jax.nn.sigmoid(conv)).astype(BF16)


def _in_proj(x2, g, wqkv, wz, wxbc, wg, wdt, qkg, cos_t, sin_t, bd, convw, convb, batch, seq):
    n_tok = x2.shape[0]
    tm = TOK_BLOCK
    blocks_per_seq = seq // tm
    hpb = tm // X_HALO
    last_halo = n_tok // X_HALO - 1
    xprev = pl.BlockSpec((X_HALO, D_MODEL), lambda i: (jnp.maximum(i * hpb - 1, 0), 0))
    xnext = pl.BlockSpec((X_HALO, D_MODEL), lambda i: (jnp.minimum((i + 1) * hpb, last_halo), 0))
    full = lambda a: pl.BlockSpec(a.shape, lambda i: (0,) * a.ndim)
    row = lambda w: pl.BlockSpec((tm, w), lambda i: (i, 0))
    tab = pl.BlockSpec((tm, 128), lambda i: (i % blocks_per_seq, 0))
    col = lambda r: pl.BlockSpec((None, r, tm),
                                 lambda i: (i // blocks_per_seq, 0, i % blocks_per_seq))
    vt_spec = pl.BlockSpec((None, N_KV_HEADS, V_ROWS, tm),
                           lambda i: (i // blocks_per_seq, 0, 0, i % blocks_per_seq))
    out_shapes = (
        jax.ShapeDtypeStruct((batch, ATTN_WIDTH, seq), BF16),
        jax.ShapeDtypeStruct((n_tok, KV_WIDTH), BF16),
        jax.ShapeDtypeStruct((batch, N_KV_HEADS, V_ROWS, seq), BF16),
        jax.ShapeDtypeStruct((n_tok, SSD_D_INNER), BF16),
        jax.ShapeDtypeStruct((n_tok, CONV_CH), BF16),
        jax.ShapeDtypeStruct((n_tok, 2 * D_MODEL), BF16),
        jax.ShapeDtypeStruct((n_tok, DT_PAD), F32),
        jax.ShapeDtypeStruct((batch, 2 * SSD_HEADS, seq), F32),
    )
    return pl.pallas_call(
        functools.partial(_in_proj_kernel, blocks_per_seq=blocks_per_seq),
        grid=(n_tok // tm,),
        in_specs=[row(D_MODEL), xprev, xnext, full(g), full(wqkv), full(wz), full(wxbc), full(wg),
                  full(wdt), full(qkg), tab, tab, full(bd), full(convw), full(convb)],
        out_specs=(col(ATTN_WIDTH), row(KV_WIDTH), vt_spec, row(SSD_D_INNER), row(CONV_CH),
                   row(2 * D_MODEL), row(DT_PAD), col(2 * SSD_HEADS)),
        out_shape=out_shapes,
        scratch_shapes=[pltpu.VMEM((tm + 2 * X_HALO, CONV_CH), F32)],
        compiler_params=pltpu.CompilerParams(dimension_semantics=("arbitrary",),
                                             vmem_limit_bytes=VMEM_LIMIT),
        name="in_proj",
    )(x2, x2, x2, g, wqkv, wz, wxbc, wg, wdt, qkg, cos_t, sin_t, bd, convw, convb)


def _sublane_allreduce(x, op):
    for shift in (4, 2, 1):
        x = op(x, pltpu.roll(x, shift, 0))
    return x


def _attn_kernel(qt_ref, k_ref, vt_ref, o_ref, qpad_scr, m_scr, acc_scr, s_scr, cmax_scr, *,
                 n_kv_blocks):
    kvh = pl.program_id(1)
    bq = qt_ref.shape[1]
    n_tiles = (Q_PER_KV * bq) // Q_TILE
    q_tile = jnp.concatenate(
        [qt_ref[g * HEAD_DIM:(g + 1) * HEAD_DIM, :] for g in range(Q_PER_KV)], axis=1)
    zero = jnp.zeros_like(q_tile)
    qpad_scr[0:HEAD_DIM, :] = jnp.where(kvh == 0, q_tile, zero)
    qpad_scr[HEAD_DIM:, :] = jnp.where(kvh == 0, zero, q_tile)
    m_scr[...] = jnp.full(m_scr.shape, -1e30, F32)
    acc_scr[...] = jnp.zeros(acc_scr.shape, F32)

    def scores(start, t):
        return _dot(k_ref[pl.ds(start, KV_BLOCK), :], qpad_scr[:, t * Q_TILE:(t + 1) * Q_TILE])

    def colmax(s):
        return _sublane_allreduce(jnp.max(s.reshape(KV_BLOCK // 8, 8, Q_TILE), axis=0), jnp.maximum)

    def softmax_pv(t, s, cmax, vt_blk):
        cols = slice(t * Q_TILE, (t + 1) * Q_TILE)
        m_prev = m_scr[:, cols]
        m_new = jnp.maximum(m_prev, cmax)
        alpha = jnp.exp2(m_prev - m_new)
        p = jnp.exp2(s.reshape(KV_BLOCK // 8, 8, Q_TILE) - m_new[None])
        pv = _dot(vt_blk, p.reshape(KV_BLOCK, Q_TILE).astype(BF16))
        acc = acc_scr[:, cols].reshape(V_ROWS // 8, 8, Q_TILE) * alpha[None]
        acc_scr[:, cols] = acc.reshape(V_ROWS, Q_TILE) + pv
        m_scr[:, cols] = m_new

    assert n_tiles >= 2
    s_scr[0] = scores(0, 0)
    s_scr[1] = scores(0, 1)
    cmax_scr[...] = colmax(s_scr[0])

    def body(j, carry):
        start = pl.multiple_of(j * KV_BLOCK, KV_BLOCK)
        nxt = pl.multiple_of(jnp.minimum(j + 1, n_kv_blocks - 1) * KV_BLOCK, KV_BLOCK)
        vt_blk = vt_ref[:, pl.ds(start, KV_BLOCK)]
        s_cur, s_nxt, cmax_cur = s_scr[0], s_scr[1], cmax_scr[...]
        for t in range(n_tiles):
            ahead = t + 2
            s_new = scores(start, ahead) if ahead < n_tiles else scores(nxt, ahead - n_tiles)
            cmax_nxt = colmax(s_nxt)
            softmax_pv(t, s_cur, cmax_cur, vt_blk)
            s_cur, s_nxt, cmax_cur = s_nxt, s_new, cmax_nxt
        s_scr[0] = s_cur
        s_scr[1] = s_nxt
        cmax_scr[...] = cmax_cur
        return carry

    lax.fori_loop(0, n_kv_blocks, body, 0, unroll=KV_UNROLL)
    inv_l = 1.0 / acc_scr[HEAD_DIM:HEAD_DIM + 8, :]
    out = acc_scr[0:HEAD_DIM, :].reshape(HEAD_DIM // 8, 8, Q_PER_KV * bq) * inv_l[None]
    out = out.reshape(HEAD_DIM, Q_PER_KV * bq)
    for g in range(Q_PER_KV):
        o_ref[g * HEAD_DIM:(g + 1) * HEAD_DIM, :] = out[:, g * bq:(g + 1) * bq].astype(BF16)


def _attention(qt, k, vt):
    b, _, s = qt.shape
    bq = Q_BLOCK
    rows = Q_PER_KV * HEAD_DIM
    return pl.pallas_call(
        functools.partial(_attn_kernel, n_kv_blocks=s // KV_BLOCK),
        grid=(b, N_KV_HEADS, s // bq),
        in_specs=[pl.BlockSpec((None, rows, bq), lambda bi, h, qi: (bi, h, qi)),
                  pl.BlockSpec((None, s, KV_WIDTH), lambda bi, h, qi: (bi, 0, 0)),
                  pl.BlockSpec((None, None, V_ROWS, s), lambda bi, h, qi: (bi, h, 0, 0))],
        out_specs=pl.BlockSpec((None, rows, bq), lambda bi, h, qi: (bi, h, qi)),
        out_shape=jax.ShapeDtypeStruct((b, ATTN_WIDTH, s), BF16),
        scratch_shapes=[pltpu.VMEM((KV_WIDTH, Q_PER_KV * bq), BF16),
                        pltpu.VMEM((8, Q_PER_KV * bq), F32),
                        pltpu.VMEM((V_ROWS, Q_PER_KV * bq), F32),
                        pltpu.VMEM((2, KV_BLOCK, Q_TILE), F32),
                        pltpu.VMEM((8, Q_TILE), F32)],
        compiler_params=pltpu.CompilerParams(
            dimension_semantics=("arbitrary", "arbitrary", "arbitrary"),
            vmem_limit_bytes=VMEM_LIMIT),
        name="attention",
    )(qt, k, vt)


def _softplus(x):
    return jnp.maximum(x, 0.0) + jnp.log1p(jnp.exp(-jnp.abs(x)))


def _ssd_direction(d, rows, act_ref, dt_ref, dtt_ref, dtb_row_ref, dtb_col_ref, alog_row_ref,
                   alog_col_ref, dskip_ref, expand_ref, h_scr, y_ref):
    hs = slice(d * SSD_HEADS, (d + 1) * SSD_HEADS)
    c_off = SSD_D_INNER + SSD_GROUPS * SSD_STATE
    b_gs = [act_ref[rows, SSD_D_INNER + g * SSD_STATE:SSD_D_INNER + (g + 1) * SSD_STATE]
            for g in range(SSD_GROUPS)]
    c_gs = [act_ref[rows, c_off + g * SSD_STATE:c_off + (g + 1) * SSD_STATE]
            for g in range(SSD_GROUPS)]

    cbs = [lax.dot_general(c_gs[g], b_gs[g], (((1,), (1,)), ((), ())), preferred_element_type=F32)
           for g in range(SSD_GROUPS)]
    h_ins = [h_scr[g] for g in range(SSD_GROUPS)]
    y_offs = [_dot(c_gs[g], h_ins[g].astype(BF16)) for g in range(SSD_GROUPS)]
    yield

    dt_c = _softplus(dt_ref[rows, hs] + dtb_row_ref[:, hs])
    dt_r = _softplus(dtt_ref[hs, rows] + dtb_col_ref[hs, :])
    adt_c = dt_c * (-jnp.exp(alog_row_ref[:, hs]))
    adt_r = dt_r * (-jnp.exp(alog_col_ref[hs, :]))
    ri = lax.broadcasted_iota(jnp.int32, (CHUNK, CHUNK), 0)
    ci = lax.broadcasted_iota(jnp.int32, (CHUNK, CHUNK), 1)
    lower = ri >= ci
    upper = ri <= ci
    before = upper if d else lower
    tri = jnp.where(before, 1.0, 0.0).astype(BF16)
    tri_t = jnp.where(lower if d else upper, 1.0, 0.0).astype(BF16)
    c_col = sum(_dot(tri, p) for p in _split_bf16(adt_c, 3))
    c_row = sum(_dot(p, tri_t) for p in _split_bf16(adt_r, 3))
    end = 0 if d else CHUNK - 1
    c_end = c_col[end:end + 1, :]
    yield

    expand = expand_ref[...]
    dt_x = _dot(dt_c.astype(BF16), expand)
    ec_x = _dot(jnp.exp(c_col).astype(BF16), expand)
    ed_x = _dot(jnp.exp(c_end - c_col).astype(BF16), expand)
    eend = jnp.broadcast_to(jnp.exp(c_end), (8, SSD_HEADS))
    eend_x = sum(_dot(p, expand) for p in _split_bf16(eend, 3))[0:1, :]
    xs = act_ref[rows, :SSD_D_INNER].astype(F32)
    xd = xs * dt_x
    xd_b = xd.astype(BF16)
    xde_b = (xd * ed_x).astype(BF16)
    yield

    pair_w = 2 * SSD_HEADDIM
    lane = lax.broadcasted_iota(jnp.int32, (CHUNK, pair_w), 1)
    zero_b = jnp.zeros((CHUNK, pair_w), BF16)
    y_pairs = []
    for hd0 in range(0, SSD_HEADS, 2):
        cb = cbs[hd0 // HEADS_PER_GROUP]
        ws = []
        for hd in (hd0, hd0 + 1):
            seg = c_col[:, hd:hd + 1] - c_row[hd:hd + 1, :]
            ws.append((cb * jnp.where(before, jnp.exp(seg), 0.0)).astype(BF16))
        xd_pair = xd_b[:, hd0 * SSD_HEADDIM:hd0 * SSD_HEADDIM + pair_w]
        diag = jnp.concatenate([jnp.where(lane < SSD_HEADDIM, xd_pair, zero_b),
                                jnp.where(lane < SSD_HEADDIM, zero_b, xd_pair)], axis=0)
        y_pairs.append(_dot(jnp.concatenate(ws, axis=1), diag))
        yield

    for g in range(SSD_GROUPS):
        gs = slice(g * GROUP_WIDTH, (g + 1) * GROUP_WIDTH)
        upd = lax.dot_general(b_gs[g], xde_b[:, gs], (((0,), (0,)), ((), ())),
                              preferred_element_type=F32)
        h_scr[g] = h_ins[g] * eend_x[:, gs] + upd
    y = jnp.concatenate(y_pairs, axis=1) + jnp.concatenate(y_offs, axis=1) * ec_x
    if d == 0:
        y = y + dskip_ref[...] * xs
    y_ref[rows, :] = y.astype(BF16)


def _ssd_kernel(actf, actb, dtf, dttf, dtb, dttb, dtb_row, dtb_col, alog_row, alog_col, dskip,
                expand, yf_ref, yb_ref, hf_scr, hb_scr):
    @pl.when(pl.program_id(1) == 0)
    def _():
        hf_scr[...] = jnp.zeros(hf_scr.shape, F32)
        hb_scr[...] = jnp.zeros(hb_scr.shape, F32)

    common = (dtb_row, dtb_col, alog_row, alog_col, dskip, expand)
    for j in range(SSD_STEP_CHUNKS):
        jb = SSD_STEP_CHUNKS - 1 - j
        stages = [_ssd_direction(0, slice(j * CHUNK, (j + 1) * CHUNK), actf, dtf, dttf, *common,
                                 hf_scr, yf_ref),
                  _ssd_direction(1, slice(jb * CHUNK, (jb + 1) * CHUNK), actb, dtb, dttb, *common,
                                 hb_scr, yb_ref)]
        while stages:
            stages = [g for g in stages if next(g, StopIteration) is not StopIteration]


def _ssd(act, dt, dtt, dtb_row, dtb_col, alog_row, alog_col, dskip, expand):
    b, s, _ = act.shape
    rows = SSD_STEP_CHUNKS * CHUNK
    nc = s // rows
    fwd = lambda i: i
    bwd = lambda i: nc - 1 - i
    act_spec = lambda c: pl.BlockSpec((None, rows, CONV_CH), lambda bi, i: (bi, c(i), 0))
    dt_spec = lambda c: pl.BlockSpec((None, rows, DT_PAD), lambda bi, i: (bi, c(i), 0))
    dtt_spec = lambda c: pl.BlockSpec((None, 2 * SSD_HEADS, rows), lambda bi, i: (bi, 0, c(i)))
    y_spec = lambda c: pl.BlockSpec((None, rows, SSD_D_INNER), lambda bi, i: (bi, c(i), 0))
    full = lambda a: pl.BlockSpec(a.shape, lambda bi, i: (0,) * a.ndim)
    consts = (dtb_row, dtb_col, alog_row, alog_col, dskip, expand)
    y_shape = jax.ShapeDtypeStruct((b, s, SSD_D_INNER), BF16)
    return pl.pallas_call(
        _ssd_kernel,
        grid=(b, nc),
        in_specs=[act_spec(fwd), act_spec(bwd), dt_spec(fwd), dtt_spec(fwd), dt_spec(bwd),
                  dtt_spec(bwd)] + [full(a) for a in consts],
        out_specs=(y_spec(fwd), y_spec(bwd)),
        out_shape=(y_shape, y_shape),
        scratch_shapes=[pltpu.VMEM((SSD_GROUPS, SSD_STATE, GROUP_WIDTH), F32),
                        pltpu.VMEM((SSD_GROUPS, SSD_STATE, GROUP_WIDTH), F32)],
        compiler_params=pltpu.CompilerParams(dimension_semantics=("arbitrary", "arbitrary"),
                                             vmem_limit_bytes=VMEM_LIMIT),
        name="ssd",
    )(act, act, dt, dtt, dt, dtt, *consts)


def _merge_kernel(x_ref, yf_ref, yb_ref, z_ref, ot_ref, gates_ref, wssd_ref, wattn_ref, wout_ref,
                  gssd_ref, gpost_ref, o_ref):
    y = yf_ref[...].astype(F32) + yb_ref[...].astype(F32)
    z = z_ref[...].astype(F32)
    ssd_in = _rms(y * (z * jax.nn.sigmoid(z)), gssd_ref[...]).astype(BF16)
    ssd_out = _dot(ssd_in, wssd_ref[...])
    attn_out = lax.dot_general(ot_ref[...], wattn_ref[...], (((0,), (0,)), ((), ())),
                               preferred_element_type=F32)
    gates = jax.nn.sigmoid(gates_ref[...].astype(F32))
    mixed = gates[:, :D_MODEL] * attn_out + gates[:, D_MODEL:] * ssd_out
    proj = _dot(mixed.astype(BF16), wout_ref[...])
    o_ref[...] = x_ref[...] + _rms(proj, gpost_ref[...])


def _merge(x2, yf, yb, z, ot, gates, wssd, wattn, wout, gssd, gpost, seq):
    n_tok = x2.shape[0]
    tm = TOK_BLOCK
    bps = seq // tm
    full = lambda a: pl.BlockSpec(a.shape, lambda i: (0,) * a.ndim)
    row = lambda w: pl.BlockSpec((tm, w), lambda i: (i, 0))
    return pl.pallas_call(
        _merge_kernel,
        grid=(n_tok // tm,),
        in_specs=[row(D_MODEL), row(SSD_D_INNER), row(SSD_D_INNER), row(SSD_D_INNER),
                  pl.BlockSpec((None, ATTN_WIDTH, tm), lambda i: (i // bps, 0, i % bps)),
                  row(2 * D_MODEL), full(wssd), full(wattn), full(wout), full(gssd), full(gpost)],
        out_specs=row(D_MODEL),
        out_shape=jax.ShapeDtypeStruct((n_tok, D_MODEL), F32),
        compiler_params=pltpu.CompilerParams(dimension_semantics=("arbitrary",),
                                             vmem_limit_bytes=VMEM_LIMIT),
        name="merge",
    )(x2, yf, yb, z, ot, gates, wssd, wattn, wout, gssd, gpost)


def _ffn_kernel(x_ref, gpre_ref, wgate_ref, wup_ref, wdown_ref, gpost_ref, o_ref):
    x = x_ref[...]
    h = _rms(x, gpre_ref[...]).astype(BF16)
    gate = _dot(h, wgate_ref[...])
    up = _dot(h, wup_ref[...])
    act = (gate * jax.nn.sigmoid(gate) * up).astype(BF16)
    ffn = _dot(act, wdown_ref[...])
    o_ref[...] = x + _rms(ffn, gpost_ref[...])


def _ffn(x2, gpre, wgate, wup, wdown, gpost):
    n_tok = x2.shape[0]
    tm = TOK_BLOCK
    full = lambda a: pl.BlockSpec(a.shape, lambda i: (0,) * a.ndim)
    row = pl.BlockSpec((tm, D_MODEL), lambda i: (i, 0))
    return pl.pallas_call(
        _ffn_kernel,
        grid=(n_tok // tm,),
        in_specs=[row, full(gpre), full(wgate), full(wup), full(wdown), full(gpost)],
        out_specs=row,
        out_shape=jax.ShapeDtypeStruct((n_tok, D_MODEL), F32),
        compiler_params=pltpu.CompilerParams(dimension_semantics=("arbitrary",),
                                             vmem_limit_bytes=VMEM_LIMIT),
        name="ffn",
    )(x2, gpre, wgate, wup, wdown, gpost)


def _rope_tables(seq):
    rows = seq // GRID_W
    row_idx = jnp.repeat(jnp.arange(rows, dtype=F32), GRID_W)
    col_idx = jnp.tile(jnp.arange(GRID_W, dtype=F32), rows)
    inv_freq = ROPE_THETA ** (-jnp.arange(0, AXIAL_DIM, 2, dtype=F32) / AXIAL_DIM)
    ang_row = row_idx[:, None] * inv_freq[None, :]
    ang_col = col_idx[:, None] * inv_freq[None, :]
    cr, sr, cc, sc = jnp.cos(ang_row), jnp.sin(ang_row), jnp.cos(ang_col), jnp.sin(ang_col)
    cos64 = jnp.concatenate([cr, cr, cc, cc], axis=-1)
    sin64 = jnp.concatenate([-sr, sr, -sc, sc], axis=-1)
    return jnp.tile(cos64, (1, 2)), jnp.tile(sin64, (1, 2))


def kernel(x, w_in, q_norm, k_norm, conv_w, conv_b, dt_bias_f, dt_bias_b, a_log_f, a_log_b, d_skip, ssd_norm, w_attn_proj, w_ssd_proj, w_out, norm1_pre, norm1_post, norm2_pre, norm2_post, w_gate_up, w_down):
    b, s, _ = x.shape
    assert w_in.shape[0] == 1, "single-layer block"
    assert s % TOK_BLOCK == 0 and s % Q_BLOCK == 0 and s % KV_BLOCK == 0
    assert s % (SSD_STEP_CHUNKS * CHUNK) == 0
    x2 = x.reshape(b * s, D_MODEL)

    w = w_in[0]
    o_z = QK_WIDTH + KV_WIDTH
    o_xbc = o_z + SSD_D_INNER
    o_dt = o_xbc + CONV_CH
    o_g = o_dt + 2 * SSD_HEADS
    wqkv = w[:, :o_z].astype(BF16)
    wz = w[:, o_z:o_xbc].astype(BF16)
    wxbc = w[:, o_xbc:o_dt].astype(BF16)
    wdt = jnp.pad(w[:, o_dt:o_g], ((0, 0), (0, DT_PAD - 2 * SSD_HEADS))).astype(BF16)
    wg = w[:, o_g:].astype(BF16)
    row = lambda v: v.reshape(1, -1).astype(F32)
    qkg = jnp.concatenate([jnp.tile(q_norm[0], N_Q_HEADS), jnp.tile(k_norm[0], N_KV_HEADS)]).reshape(1, -1)
    cos_t, sin_t = _rope_tables(s)
    seg = np.arange(128) // HEAD_DIM
    bd = jnp.asarray(seg[:, None] == seg[None, :], dtype=BF16)

    convw = jnp.pad(conv_w[0].astype(F32), ((0, 8 - CONV_K), (0, 0)))
    qt, k, vt, z, act, gates, dt, dtt = _in_proj(x2, row(norm1_pre[0]), wqkv, wz, wxbc, wg, wdt,
                                                 qkg, cos_t, sin_t, bd, convw, row(conv_b[0]), b, s)
    ot = _attention(qt, k.reshape(b, s, KV_WIDTH), vt)

    dt3 = dt.reshape(b, s, DT_PAD)
    dt_bias = jnp.concatenate([dt_bias_f[0], dt_bias_b[0]]).astype(F32)
    a_log = jnp.concatenate([a_log_f[0], a_log_b[0]]).astype(F32)
    dskip = jnp.repeat(d_skip[0].astype(F32), SSD_HEADDIM).reshape(1, -1)
    hid = np.arange(SSD_D_INNER) // SSD_HEADDIM
    expand = jnp.asarray(np.arange(SSD_HEADS)[:, None] == hid[None, :], dtype=BF16)
    yf, yb = _ssd(act.reshape(b, s, CONV_CH), dt3, dtt, dt_bias.reshape(1, -1), dt_bias.reshape(-1, 1),
                  a_log.reshape(1, -1), a_log.reshape(-1, 1), dskip, expand)

    x1 = _merge(x2, yf.reshape(b * s, -1), yb.reshape(b * s, -1), z, ot, gates,
                w_ssd_proj[0].astype(BF16), w_attn_proj[0].astype(BF16), w_out[0].astype(BF16),
                row(ssd_norm[0]), row(norm1_post[0]), s)

    wgu = w_gate_up[0]
    out = _ffn(x1, row(norm2_pre[0]), wgu[:, :D_FF].astype(BF16), wgu[:, D_FF:].astype(BF16),
               w_down[0].astype(BF16), row(norm2_post[0]))
    return out.reshape(b, s, D_MODEL)
```

```python
import functools
import math

import jax
import jax.numpy as jnp
import numpy as np
from jax import lax
from jax.experimental import pallas as pl
from jax.experimental.pallas import tpu as pltpu

F32 = jnp.float32
BF16 = jnp.bfloat16

D_MODEL = 1024
GRID_W = 64
HEAD_DIM = 64
N_Q_HEADS = 8
N_KV_HEADS = 2
Q_PER_KV = N_Q_HEADS // N_KV_HEADS
ATTN_WIDTH = N_Q_HEADS * HEAD_DIM
KV_WIDTH = N_KV_HEADS * HEAD_DIM
QK_WIDTH = ATTN_WIDTH + KV_WIDTH
ROPE_THETA = 10000.0
AXIAL_DIM = HEAD_DIM // 2
SSD_D_INNER = D_MODEL
SSD_HEADDIM = 64
SSD_HEADS = SSD_D_INNER // SSD_HEADDIM
SSD_GROUPS = 2
HEADS_PER_GROUP = SSD_HEADS // SSD_GROUPS
GROUP_WIDTH = HEADS_PER_GROUP * SSD_HEADDIM
SSD_STATE = 128
CONV_K = 5
CONV_PAD = (CONV_K - 1) // 2
CHUNK = 128
SSD_STEP_CHUNKS = 4
CONV_CH = SSD_D_INNER + 2 * SSD_GROUPS * SSD_STATE
D_FF = ((8 * D_MODEL // 3 + 255) // 256) * 256
EPS = 1e-6
DT_PAD = 128
X_HALO = 8

VMEM_LIMIT = 56 * 1024 * 1024

TOK_BLOCK = 512
Q_BLOCK = 1024
KV_BLOCK = 256
Q_TILE = 512
KV_UNROLL = 64
ONES_ROWS = 16
V_ROWS = HEAD_DIM + ONES_ROWS


def _rms(xf, gain):
    return xf * lax.rsqrt(jnp.mean(xf * xf, axis=-1, keepdims=True) + EPS) * gain


def _split_bf16(a, parts):
    out = []
    r = a
    for _ in range(parts):
        p = r.astype(BF16)
        out.append(p)
        r = r - p.astype(F32)
    return out


def _dot(a, b):
    return jnp.dot(a, b, preferred_element_type=F32)


def _in_proj_kernel(x_ref, xprev_ref, xnext_ref, g_ref, wqkv_ref, wz_ref, wxbc_ref, wg_ref, wdt_ref,
                    qkg_ref, cos_ref, sin_ref, bd_ref, convw_ref, convb_ref, qt_ref, k_ref, vt_ref,
                    z_ref, act_ref, gates_ref, dt_ref, dtt_ref, ext_scr, *, blocks_per_seq):
    tm = x_ref.shape[0]
    h = _rms(x_ref[...], g_ref[...]).astype(BF16)

    pos = pl.program_id(0) % blocks_per_seq
    x_halo = jnp.concatenate([xprev_ref[...], xnext_ref[...]], axis=0)
    halo = _dot(_rms(x_halo, g_ref[...]).astype(BF16), wxbc_ref[...])
    ext_scr[0:X_HALO, :] = halo[:X_HALO] * (pos > 0).astype(F32)
    ext_scr[X_HALO:X_HALO + tm, :] = _dot(h, wxbc_ref[...])
    ext_scr[X_HALO + tm:, :] = halo[X_HALO:] * (pos < blocks_per_seq - 1).astype(F32)

    z_ref[...] = _dot(h, wz_ref[...]).astype(BF16)
    gates_ref[...] = _dot(h, wg_ref[...]).astype(BF16)
    dt = _dot(h, wdt_ref[...])
    dt_ref[...] = dt
    dtt_ref[...] = dt.T[:2 * SSD_HEADS, :]
    qkv = _dot(h, wqkv_ref[...])
    vt = qkv[:, QK_WIDTH:].T
    for hd in range(N_KV_HEADS):
        vt_ref[hd, 0:HEAD_DIM, :] = vt[hd * HEAD_DIM:(hd + 1) * HEAD_DIM, :].astype(BF16)
        vt_ref[hd, HEAD_DIM:, :] = jnp.ones((ONES_ROWS, vt.shape[1]), BF16)

    qk = qkv[:, :QK_WIDTH]
    sq = (qk * qk).astype(BF16)
    bd = bd_ref[...]
    n_lane_tiles = QK_WIDTH // 128
    ms = jnp.concatenate([_dot(sq[:, i * 128:(i + 1) * 128], bd) for i in range(n_lane_tiles)],
                         axis=1) * (1.0 / HEAD_DIM)
    qn = qk * lax.rsqrt(ms + EPS) * qkg_ref[...]
    half = AXIAL_DIM // 2
    lane = lax.broadcasted_iota(jnp.int32, qn.shape, 1)
    partner = jnp.where((lane % AXIAL_DIM) < half,
                        pltpu.roll(qn, QK_WIDTH - half, 1), pltpu.roll(qn, half, 1))
    cos = jnp.concatenate([cos_ref[...]] * n_lane_tiles, axis=1)
    sin = jnp.concatenate([sin_ref[...]] * n_lane_tiles, axis=1)
    roped = qn * cos + partner * sin
    q = roped[:, :ATTN_WIDTH] * (math.log2(math.e) / math.sqrt(HEAD_DIM))
    qt_ref[...] = q.T.astype(BF16)
    k_ref[...] = roped[:, ATTN_WIDTH:].astype(BF16)

    conv = jnp.broadcast_to(convb_ref[...], (tm, CONV_CH))
    for k in range(CONV_K):
        conv = conv + convw_ref[k:k + 1, :] * ext_scr[pl.ds(X_HALO - CONV_PAD + k, tm), :]
    act_ref[...] = (conv * jax.nn.sigmoid(conv)).astype(BF16)


def _in_proj(x2, g, wqkv, wz, wxbc, wg, wdt, qkg, cos_t, sin_t, bd, convw, convb, batch, seq):
    n_tok = x2.shape[0]
    tm = TOK_BLOCK
    blocks_per_seq = seq // tm
    hpb = tm // X_HALO
    last_halo = n_tok // X_HALO - 1
    xprev = pl.BlockSpec((X_HALO, D_MODEL), lambda i: (jnp.maximum(i * hpb - 1, 0), 0))
    xnext = pl.BlockSpec((X_HALO, D_MODEL), lambda i: (jnp.minimum((i + 1) * hpb, last_halo), 0))
    full = lambda a: pl.BlockSpec(a.shape, lambda i: (0,) * a.ndim)
    row = lambda w: pl.BlockSpec((tm, w), lambda i: (i, 0))
    tab = pl.BlockSpec((tm, 128), lambda i: (i % blocks_per_seq, 0))
    col = lambda r: pl.BlockSpec((None, r, tm),
                                 lambda i: (i // blocks_per_seq, 0, i % blocks_per_seq))
    vt_spec = pl.BlockSpec((None, N_KV_HEADS, V_ROWS, tm),
                           lambda i: (i // blocks_per_seq, 0, 0, i % blocks_per_seq))
    out_shapes = (
        jax.ShapeDtypeStruct((batch, ATTN_WIDTH, seq), BF16),
        jax.ShapeDtypeStruct((n_tok, KV_WIDTH), BF16),
        jax.ShapeDtypeStruct((batch, N_KV_HEADS, V_ROWS, seq), BF16),
        jax.ShapeDtypeStruct((n_tok, SSD_D_INNER), BF16),
        jax.ShapeDtypeStruct((n_tok, CONV_CH), BF16),
        jax.ShapeDtypeStruct((n_tok, 2 * D_MODEL), BF16),
        jax.ShapeDtypeStruct((n_tok, DT_PAD), F32),
        jax.ShapeDtypeStruct((batch, 2 * SSD_HEADS, seq), F32),
    )
    return pl.pallas_call(
        functools.partial(_in_proj_kernel, blocks_per_seq=blocks_per_seq),
        grid=(n_tok // tm,),
        in_specs=[row(D_MODEL), xprev, xnext, full(g), full(wqkv), full(wz), full(wxbc), full(wg),
                  full(wdt), full(qkg), tab, tab, full(bd), full(convw), full(convb)],
        out_specs=(col(ATTN_WIDTH), row(KV_WIDTH), vt_spec, row(SSD_D_INNER), row(CONV_CH),
                   row(2 * D_MODEL), row(DT_PAD), col(2 * SSD_HEADS)),
        out_shape=out_shapes,
        scratch_shapes=[pltpu.VMEM((tm + 2 * X_HALO, CONV_CH), F32)],
        compiler_params=pltpu.CompilerParams(dimension_semantics=("arbitrary",),
                                             vmem_limit_bytes=VMEM_LIMIT),
        name="in_proj",
    )(x2, x2, x2, g, wqkv, wz, wxbc, wg, wdt, qkg, cos_t, sin_t, bd, convw, convb)


def _sublane_allreduce(x, op):
    for shift in (4, 2, 1):
        x = op(x, pltpu.roll(x, shift, 0))
    return x


def _attn_kernel(qt_ref, k_ref, vt_ref, o_ref, qpad_scr, m_scr, acc_scr, s_scr, cmax_scr, *,
                 n_kv_blocks):
    kvh = pl.program_id(1)
    bq = qt_ref.shape[1]
    n_tiles = (Q_PER_KV * bq) // Q_TILE
    q_tile = jnp.concatenate(
        [qt_ref[g * HEAD_DIM:(g + 1) * HEAD_DIM, :] for g in range(Q_PER_KV)], axis=1)
    zero = jnp.zeros_like(q_tile)
    qpad_scr[0:HEAD_DIM, :] = jnp.where(kvh == 0, q_tile, zero)
    qpad_scr[HEAD_DIM:, :] = jnp.where(kvh == 0, zero, q_tile)
    m_scr[...] = jnp.full(m_scr.shape, -1e30, F32)
    acc_scr[...] = jnp.zeros(acc_scr.shape, F32)

    def scores(start, t):
        return _dot(k_ref[pl.ds(start, KV_BLOCK), :], qpad_scr[:, t * Q_TILE:(t + 1) * Q_TILE])

    def colmax(s):
        return _sublane_allreduce(jnp.max(s.reshape(KV_BLOCK // 8, 8, Q_TILE), axis=0), jnp.maximum)

    def softmax_pv(t, s, cmax, vt_blk):
        cols = slice(t * Q_TILE, (t + 1) * Q_TILE)
        m_prev = m_scr[:, cols]
        m_new = jnp.maximum(m_prev, cmax)
        alpha = jnp.exp2(m_prev - m_new)
        p = jnp.exp2(s.reshape(KV_BLOCK // 8, 8, Q_TILE) - m_new[None])
        pv = _dot(vt_blk, p.reshape(KV_BLOCK, Q_TILE).astype(BF16))
        acc = acc_scr[:, cols].reshape(V_ROWS // 8, 8, Q_TILE) * alpha[None]
        acc_scr[:, cols] = acc.reshape(V_ROWS, Q_TILE) + pv
        m_scr[:, cols] = m_new

    assert n_tiles >= 2
    s_scr[0] = scores(0, 0)
    s_scr[1] = scores(0, 1)
    cmax_scr[...] = colmax(s_scr[0])

    def body(j, carry):
        start = pl.multiple_of(j * KV_BLOCK, KV_BLOCK)
        nxt = pl.multiple_of(jnp.minimum(j + 1, n_kv_blocks - 1) * KV_BLOCK, KV_BLOCK)
        vt_blk = vt_ref[:, pl.ds(start, KV_BLOCK)]
        s_cur, s_nxt, cmax_cur = s_scr[0], s_scr[1], cmax_scr[...]
        for t in range(n_tiles):
            ahead = t + 2
            s_new = scores(start, ahead) if ahead < n_tiles else scores(nxt, ahead - n_tiles)
            cmax_nxt = colmax(s_nxt)
            softmax_pv(t, s_cur, cmax_cur, vt_blk)
            s_cur, s_nxt, cmax_cur = s_nxt, s_new, cmax_nxt
        s_scr[0] = s_cur
        s_scr[1] = s_nxt
        cmax_scr[...] = cmax_cur
        return carry

    lax.fori_loop(0, n_kv_blocks, body, 0, unroll=KV_UNROLL)
    inv_l = 1.0 / acc_scr[HEAD_DIM:HEAD_DIM + 8, :]
    out = acc_scr[0:HEAD_DIM, :].reshape(HEAD_DIM // 8, 8, Q_PER_KV * bq) * inv_l[None]
    out = out.reshape(HEAD_DIM, Q_PER_KV * bq)
    for g in range(Q_PER_KV):
        o_ref[g * HEAD_DIM:(g + 1) * HEAD_DIM, :] = out[:, g * bq:(g + 1) * bq].astype(BF16)


def _attention(qt, k, vt):
    b, _, s = qt.shape
    bq = Q_BLOCK
    rows = Q_PER_KV * HEAD_DIM
    return pl.pallas_call(
        functools.partial(_attn_kernel, n_kv_blocks=s // KV_BLOCK),
        grid=(b, N_KV_HEADS, s // bq),
        in_specs=[pl.BlockSpec((None, rows, bq), lambda bi, h, qi: (bi, h, qi)),
                  pl.BlockSpec((None, s, KV_WIDTH), lambda bi, h, qi: (bi, 0, 0)),
                  pl.BlockSpec((None, None, V_ROWS, s), lambda bi, h, qi: (bi, h, 0, 0))],
        out_specs=pl.BlockSpec((None, rows, bq), lambda bi, h, qi: (bi, h, qi)),
        out_shape=jax.ShapeDtypeStruct((b, ATTN_WIDTH, s), BF16),
        scratch_shapes=[pltpu.VMEM((KV_WIDTH, Q_PER_KV * bq), BF16),
                        pltpu.VMEM((8, Q_PER_KV * bq), F32),
                        pltpu.VMEM((V_ROWS, Q_PER_KV * bq), F32),
                        pltpu.VMEM((2, KV_BLOCK, Q_TILE), F32),
                        pltpu.VMEM((8, Q_TILE), F32)],
        compiler_params=pltpu.CompilerParams(
            dimension_semantics=("arbitrary", "arbitrary", "arbitrary"),
            vmem_limit_bytes=VMEM_LIMIT),
        name="attention",
    )(qt, k, vt)


def _softplus(x):
    return jnp.maximum(x, 0.0) + jnp.log1p(jnp.exp(-jnp.abs(x)))


def _ssd_direction(d, rows, act_ref, dt_ref, dtt_ref, dtb_row_ref, dtb_col_ref, alog_row_ref,
                   alog_col_ref, dskip_ref, expand_ref, h_scr, y_ref):
    hs = slice(d * SSD_HEADS, (d + 1) * SSD_HEADS)
    c_off = SSD_D_INNER + SSD_GROUPS * SSD_STATE
    b_gs = [act_ref[rows, SSD_D_INNER + g * SSD_STATE:SSD_D_INNER + (g + 1) * SSD_STATE]
            for g in range(SSD_GROUPS)]
    c_gs = [act_ref[rows, c_off + g * SSD_STATE:c_off + (g + 1) * SSD_STATE]
            for g in range(SSD_GROUPS)]

    cbs = [lax.dot_general(c_gs[g], b_gs[g], (((1,), (1,)), ((), ())), preferred_element_type=F32)
           for g in range(SSD_GROUPS)]
    h_ins = [h_scr[g] for g in range(SSD_GROUPS)]
    y_offs = [_dot(c_gs[g], h_ins[g].astype(BF16)) for g in range(SSD_GROUPS)]
    yield

    dt_c = _softplus(dt_ref[rows, hs] + dtb_row_ref[:, hs])
    dt_r = _softplus(dtt_ref[hs, rows] + dtb_col_ref[hs, :])
    adt_c = dt_c * (-jnp.exp(alog_row_ref[:, hs]))
    adt_r = dt_r * (-jnp.exp(alog_col_ref[hs, :]))
    ri = lax.broadcasted_iota(jnp.int32, (CHUNK, CHUNK), 0)
    ci = lax.broadcasted_iota(jnp.int32, (CHUNK, CHUNK), 1)
    lower = ri >= ci
    upper = ri <= ci
    before = upper if d else lower
    tri = jnp.where(before, 1.0, 0.0).astype(BF16)
    tri_t = jnp.where(lower if d else upper, 1.0, 0.0).astype(BF16)
    c_col = sum(_dot(tri, p) for p in _split_bf16(adt_c, 3))
    c_row = sum(_dot(p, tri_t) for p in _split_bf16(adt_r, 3))
    end = 0 if d else CHUNK - 1
    c_end = c_col[end:end + 1, :]
    yield

    expand = expand_ref[...]
    dt_x = _dot(dt_c.astype(BF16), expand)
    ec_x = _dot(jnp.exp(c_col).astype(BF16), expand)
    ed_x = _dot(jnp.exp(c_end - c_col).astype(BF16), expand)
    eend = jnp.broadcast_to(jnp.exp(c_end), (8, SSD_HEADS))
    eend_x = sum(_dot(p, expand) for p in _split_bf16(eend, 3))[0:1, :]
    xs = act_ref[rows, :SSD_D_INNER].astype(F32)
    xd = xs * dt_x
    xd_b = xd.astype(BF16)
    xde_b = (xd * ed_x).astype(BF16)
    yield

    pair_w = 2 * SSD_HEADDIM
    lane = lax.broadcasted_iota(jnp.int32, (CHUNK, pair_w), 1)
    zero_b = jnp.zeros((CHUNK, pair_w), BF16)
    y_pairs = []
    for hd0 in range(0, SSD_HEADS, 2):
        cb = cbs[hd0 // HEADS_PER_GROUP]
        ws = []
        for hd in (hd0, hd0 + 1):
            seg = c_col[:, hd:hd + 1] - c_row[hd:hd + 1, :]
            ws.append((cb * jnp.where(before, jnp.exp(seg), 0.0)).astype(BF16))
        xd_pair = xd_b[:, hd0 * SSD_HEADDIM:hd0 * SSD_HEADDIM + pair_w]
        diag = jnp.concatenate([jnp.where(lane < SSD_HEADDIM, xd_pair, zero_b),
                                jnp.where(lane < SSD_HEADDIM, zero_b, xd_pair)], axis=0)
        y_pairs.append(_dot(jnp.concatenate(ws, axis=1), diag))
        yield

    for g in range(SSD_GROUPS):
        gs = slice(g * GROUP_WIDTH, (g + 1) * GROUP_WIDTH)
        upd = lax.dot_general(b_gs[g], xde_b[:, gs], (((0,), (0,)), ((), ())),
                              preferred_element_type=F32)
        h_scr[g] = h_ins[g] * eend_x[:, gs] + upd
    y = jnp.concatenate(y_pairs, axis=1) + jnp.concatenate(y_offs, axis=1) * ec_x
    if d == 0:
        y = y + dskip_ref[...] * xs
    y_ref[rows, :] = y.astype(BF16)


def _ssd_kernel(actf, actb, dtf, dttf, dtb, dttb, dtb_row, dtb_col, alog_row, alog_col, dskip,
                expand, yf_ref, yb_ref, hf_scr, hb_scr):
    @pl.when(pl.program_id(1) == 0)
    def _():
        hf_scr[...] = jnp.zeros(hf_scr.shape, F32)
        hb_scr[...] = jnp.zeros(hb_scr.shape, F32)

    common = (dtb_row, dtb_col, alog_row, alog_col, dskip, expand)
    for j in range(SSD_STEP_CHUNKS):
        jb = SSD_STEP_CHUNKS - 1 - j
        stages = [_ssd_direction(0, slice(j * CHUNK, (j + 1) * CHUNK), actf, dtf, dttf, *common,
                                 hf_scr, yf_ref),
                  _ssd_direction(1, slice(jb * CHUNK, (jb + 1) * CHUNK), actb, dtb, dttb, *common,
                                 hb_scr, yb_ref)]
        while stages:
            stages = [g for g in stages if next(g, StopIteration) is not StopIteration]


def _ssd(act, dt, dtt, dtb_row, dtb_col, alog_row, alog_col, dskip, expand):
    b, s, _ = act.shape
    rows = SSD_STEP_CHUNKS * CHUNK
    nc = s // rows
    fwd = lambda i: i
    bwd = lambda i: nc - 1 - i
    act_spec = lambda c: pl.BlockSpec((None, rows, CONV_CH), lambda bi, i: (bi, c(i), 0))
    dt_spec = lambda c: pl.BlockSpec((None, rows, DT_PAD), lambda bi, i: (bi, c(i), 0))
    dtt_spec = lambda c: pl.BlockSpec((None, 2 * SSD_HEADS, rows), lambda bi, i: (bi, 0, c(i)))
    y_spec = lambda c: pl.BlockSpec((None, rows, SSD_D_INNER), lambda bi, i: (bi, c(i), 0))
    full = lambda a: pl.BlockSpec(a.shape, lambda bi, i: (0,) * a.ndim)
    consts = (dtb_row, dtb_col, alog_row, alog_col, dskip, expand)
    y_shape = jax.ShapeDtypeStruct((b, s, SSD_D_INNER), BF16)
    return pl.pallas_call(
        _ssd_kernel,
        grid=(b, nc),
        in_specs=[act_spec(fwd), act_spec(bwd), dt_spec(fwd), dtt_spec(fwd), dt_spec(bwd),
                  dtt_spec(bwd)] + [full(a) for a in consts],
        out_specs=(y_spec(fwd), y_spec(bwd)),
        out_shape=(y_shape, y_shape),
        scratch_shapes=[pltpu.VMEM((SSD_GROUPS, SSD_STATE, GROUP_WIDTH), F32),
                        pltpu.VMEM((SSD_GROUPS, SSD_STATE, GROUP_WIDTH), F32)],
        compiler_params=pltpu.CompilerParams(dimension_semantics=("arbitrary", "arbitrary"),
                                             vmem_limit_bytes=VMEM_LIMIT),
        name="ssd",
    )(act, act, dt, dtt, dt, dtt, *consts)


def _merge_kernel(x_ref, yf_ref, yb_ref, z_ref, ot_ref, gates_ref, wssd_ref, wattn_ref, wout_ref,
                  gssd_ref, gpost_ref, o_ref):
    y = yf_ref[...].astype(F32) + yb_ref[...].astype(F32)
    z = z_ref[...].astype(F32)
    ssd_in = _rms(y * (z * jax.nn.sigmoid(z)), gssd_ref[...]).astype(BF16)
    ssd_out = _dot(ssd_in, wssd_ref[...])
    attn_out = lax.dot_general(ot_ref[...], wattn_ref[...], (((0,), (0,)), ((), ())),
                               preferred_element_type=F32)
    gates = jax.nn.sigmoid(gates_ref[...].astype(F32))
    mixed = gates[:, :D_MODEL] * attn_out + gates[:, D_MODEL:] * ssd_out
    proj = _dot(mixed.astype(BF16), wout_ref[...])
    o_ref[...] = x_ref[...] + _rms(proj, gpost_ref[...])


def _merge(x2, yf, yb, z, ot, gates, wssd, wattn, wout, gssd, gpost, seq):
    n_tok = x2.shape[0]
    tm = TOK_BLOCK
    bps = seq // tm
    full = lambda a: pl.BlockSpec(a.shape, lambda i: (0,) * a.ndim)
    row = lambda w: pl.BlockSpec((tm, w), lambda i: (i, 0))
    return pl.pallas_call(
        _merge_kernel,
        grid=(n_tok // tm,),
        in_specs=[row(D_MODEL), row(SSD_D_INNER), row(SSD_D_INNER), row(SSD_D_INNER),
                  pl.BlockSpec((None, ATTN_WIDTH, tm), lambda i: (i // bps, 0, i % bps)),
                  row(2 * D_MODEL), full(wssd), full(wattn), full(wout), full(gssd), full(gpost)],
        out_specs=row(D_MODEL),
        out_shape=jax.ShapeDtypeStruct((n_tok, D_MODEL), F32),
        compiler_params=pltpu.CompilerParams(dimension_semantics=("arbitrary",),
                                             vmem_limit_bytes=VMEM_LIMIT),
        name="merge",
    )(x2, yf, yb, z, ot, gates, wssd, wattn, wout, gssd, gpost)


def _ffn_kernel(x_ref, gpre_ref, wgate_ref, wup_ref, wdown_ref, gpost_ref, o_ref):
    x = x_ref[...]
    h = _rms(x, gpre_ref[...]).astype(BF16)
    gate = _dot(h, wgate_ref[...])
    up = _dot(h, wup_ref[...])
    act = (gate * jax.nn.sigmoid(gate) * up).astype(BF16)
    ffn = _dot(act, wdown_ref[...])
    o_ref[...] = x + _rms(ffn, gpost_ref[...])


def _ffn(x2, gpre, wgate, wup, wdown, gpost):
    n_tok = x2.shape[0]
    tm = TOK_BLOCK
    full = lambda a: pl.BlockSpec(a.shape, lambda i: (0,) * a.ndim)
    row = pl.BlockSpec((tm, D_MODEL), lambda i: (i, 0))
    return pl.pallas_call(
        _ffn_kernel,
        grid=(n_tok // tm,),
        in_specs=[row, full(gpre), full(wgate), full(wup), full(wdown), full(gpost)],
        out_specs=row,
        out_shape=jax.ShapeDtypeStruct((n_tok, D_MODEL), F32),
        compiler_params=pltpu.CompilerParams(dimension_semantics=("arbitrary",),
                                             vmem_limit_bytes=VMEM_LIMIT),
        name="ffn",
    )(x2, gpre, wgate, wup, wdown, gpost)


def _rope_tables(seq):
    rows = seq // GRID_W
    row_idx = jnp.repeat(jnp.arange(rows, dtype=F32), GRID_W)
    col_idx = jnp.tile(jnp.arange(GRID_W, dtype=F32), rows)
    inv_freq = ROPE_THETA ** (-jnp.arange(0, AXIAL_DIM, 2, dtype=F32) / AXIAL_DIM)
    ang_row = row_idx[:, None] * inv_freq[None, :]
    ang_col = col_idx[:, None] * inv_freq[None, :]
    cr, sr, cc, sc = jnp.cos(ang_row), jnp.sin(ang_row), jnp.cos(ang_col), jnp.sin(ang_col)
    cos64 = jnp.concatenate([cr, cr, cc, cc], axis=-1)
    sin64 = jnp.concatenate([-sr, sr, -sc, sc], axis=-1)
    return jnp.tile(cos64, (1, 2)), jnp.tile(sin64, (1, 2))


def kernel(x, w_in, q_norm, k_norm, conv_w, conv_b, dt_bias_f, dt_bias_b, a_log_f, a_log_b, d_skip, ssd_norm, w_attn_proj, w_ssd_proj, w_out, norm1_pre, norm1_post, norm2_pre, norm2_post, w_gate_up, w_down):
    b, s, _ = x.shape
    assert w_in.shape[0] == 1, "single-layer block"
    assert s % TOK_BLOCK == 0 and s % Q_BLOCK == 0 and s % KV_BLOCK == 0
    assert s % (SSD_STEP_CHUNKS * CHUNK) == 0
    x2 = x.reshape(b * s, D_MODEL)

    w = w_in[0]
    o_z = QK_WIDTH + KV_WIDTH
    o_xbc = o_z + SSD_D_INNER
    o_dt = o_xbc + CONV_CH
    o_g = o_dt + 2 * SSD_HEADS
    wqkv = w[:, :o_z].astype(BF16)
    wz = w[:, o_z:o_xbc].astype(BF16)
    wxbc = w[:, o_xbc:o_dt].astype(BF16)
    wdt = jnp.pad(w[:, o_dt:o_g], ((0, 0), (0, DT_PAD - 2 * SSD_HEADS))).astype(BF16)
    wg = w[:, o_g:].astype(BF16)
    row = lambda v: v.reshape(1, -1).astype(F32)
    qkg = jnp.concatenate([jnp.tile(q_norm[0], N_Q_HEADS), jnp.tile(k_norm[0], N_KV_HEADS)]).reshape(1, -1)
    cos_t, sin_t = _rope_tables(s)
    seg = np.arange(128) // HEAD_DIM
    bd = jnp.asarray(seg[:, None] == seg[None, :], dtype=BF16)

    convw = jnp.pad(conv_w[0].astype(F32), ((0, 8 - CONV_K), (0, 0)))
    qt, k, vt, z, act, gates, dt, dtt = _in_proj(x2, row(norm1_pre[0]), wqkv, wz, wxbc, wg, wdt,
                                                 qkg, cos_t, sin_t, bd, convw, row(conv_b[0]), b, s)
    ot = _attention(qt, k.reshape(b, s, KV_WIDTH), vt)

    dt3 = dt.reshape(b, s, DT_PAD)
    dt_bias = jnp.concatenate([dt_bias_f[0], dt_bias_b[0]]).astype(F32)
    a_log = jnp.concatenate([a_log_f[0], a_log_b[0]]).astype(F32)
    dskip = jnp.repeat(d_skip[0].astype(F32), SSD_HEADDIM).reshape(1, -1)
    hid = np.arange(SSD_D_INNER) // SSD_HEADDIM
    expand = jnp.asarray(np.arange(SSD_HEADS)[:, None] == hid[None, :], dtype=BF16)
    yf, yb = _ssd(act.reshape(b, s, CONV_CH), dt3, dtt, dt_bias.reshape(1, -1), dt_bias.reshape(-1, 1),
                  a_log.reshape(1, -1), a_log.reshape(-1, 1), dskip, expand)

    x1 = _merge(x2, yf.reshape(b * s, -1), yb.reshape(b * s, -1), z, ot, gates,
                w_ssd_proj[0].astype(BF16), w_attn_proj[0].astype(BF16), w_out[0].astype(BF16),
                row(ssd_norm[0]), row(norm1_post[0]), s)

    wgu = w_gate_up[0]
    out = _ffn(x1, row(norm2_pre[0]), wgu[:, :D_FF].astype(BF16), wgu[:, D_FF:].astype(BF16),
               w_down[0].astype(BF16), row(norm2_post[0]))
    return out.reshape(b, s, D_MODEL)
```
